```python
import math, functools
import jax, jax.numpy as jnp
from jax import lax
import numpy as np

D_MODEL = 1024
BATCH = 4
SEQ = 4096
DEPTH = 1
DEC_BATCH = 128
DEC_SEQ = 8
PAST_LEN = 16384
PAGE_SIZE = 128

MLA_HEADS = 4
MLA_NOPE_DIM = 128
MLA_ROPE_DIM = 64
MLA_V_DIM = 128
Q_LORA_RANK = 384
KV_LORA_RANK = 256
ROPE_BASE = 10000.0
MLA_SCALE = (MLA_NOPE_DIM + MLA_ROPE_DIM) ** -0.5
DIFF_HEADS = 4
DIFF_HEAD_DIM = 64
DIFF_SCALE = DIFF_HEAD_DIM ** -0.5
DIFF_WIDTH = DIFF_HEADS * 2 * DIFF_HEAD_DIM
MIX_WIDTH = MLA_HEADS * MLA_V_DIM + DIFF_WIDTH
IN_COLS = Q_LORA_RANK + KV_LORA_RANK + MLA_ROPE_DIM + 3 * DIFF_WIDTH
MEM_LEN = 256
MEM_HEADS = 4
MEM_HEAD_DIM = D_MODEL // MEM_HEADS
N_EXPERTS = 32
TOP_K = 4
D_FF = D_MODEL
SWIGLU_ALPHA = 1.702
SWIGLU_LIMIT = 7.0
Q_BLOCK = 128
NORM_EPS = 1e-6
NEG_INF = -1e30

kernel_name = 'hymba_mla_diffattn_alibi_memxattn_moe_step'


def rms_norm(x, g):
    xf = x.astype(jnp.float32)
    y = xf * lax.rsqrt(jnp.mean(xf * xf, axis=-1, keepdims=True) + NORM_EPS)
    return (y * g.astype(jnp.float32)).astype(x.dtype)


def rope(x, pos):
    half = x.shape[-1] // 2
    inv = ROPE_BASE ** (-jnp.arange(half, dtype=jnp.float32) / half)
    ang = pos.astype(jnp.float32)[..., None] * inv
    cos, sin = jnp.cos(ang), jnp.sin(ang)
    x1 = x[..., :half].astype(jnp.float32)
    x2 = x[..., half:].astype(jnp.float32)
    return jnp.concatenate([x1 * cos - x2 * sin, x2 * cos + x1 * sin], axis=-1).astype(x.dtype)


def alibi_slopes(n_heads):
    return 2.0 ** (-8.0 * (jnp.arange(n_heads, dtype=jnp.float32) + 1.0) / n_heads)


def mla_core(q_lat, q_rope, c_kv, k_rope, q_pos, k_pos):
    s = (jnp.einsum('bqhr,bkr->bhqk', q_lat, c_kv)
         + jnp.einsum('bqhp,bkp->bhqk', q_rope, k_rope)).astype(jnp.float32) * MLA_SCALE
    s = jnp.where(k_pos[None, :] <= q_pos[:, None], s, NEG_INF)
    p = jax.nn.softmax(s, axis=-1).astype(c_kv.dtype)
    return jnp.einsum('bhqk,bkr->bqhr', p, c_kv)


def diff_core(q, k, v, q_pos, k_pos, lam):
    d = DIFF_HEAD_DIM
    dist = jnp.abs(q_pos[:, None] - k_pos[None, :]).astype(jnp.float32)
    bias = -alibi_slopes(DIFF_HEADS)[:, None, None] * dist
    mask = k_pos[None, :] <= q_pos[:, None]

    def probs(qa, ka):
        s = jnp.einsum('bqhd,bkhd->bhqk', qa, ka).astype(jnp.float32) * DIFF_SCALE + bias
        return jax.nn.softmax(jnp.where(mask, s, NEG_INF), axis=-1)

    p = probs(q[..., :d], k[..., :d]) - lam * probs(q[..., d:], k[..., d:])
    return jnp.einsum('bhqk,bkhe->bqhe', p.astype(v.dtype), v)


def mixer_inputs(h, pos, w_in, q_norm, w_uq, kv_norm, w_ukv):
    b, s, _ = h.shape
    sizes = [Q_LORA_RANK, KV_LORA_RANK, MLA_ROPE_DIM, DIFF_WIDTH, DIFF_WIDTH, DIFF_WIDTH]
    cq, ckv, kr, dq, dk, dv = jnp.split(h @ w_in, np.cumsum(sizes)[:-1].tolist(), axis=-1)
    q = (rms_norm(cq, q_norm) @ w_uq).reshape(b, s, MLA_HEADS, MLA_NOPE_DIM + MLA_ROPE_DIM)
    w_uk = w_ukv.reshape(KV_LORA_RANK, MLA_HEADS, MLA_NOPE_DIM + MLA_V_DIM)[..., :MLA_NOPE_DIM]
    q_lat = jnp.einsum('bshn,rhn->bshr', q[..., :MLA_NOPE_DIM], w_uk)
    q_rope = rope(q[..., MLA_NOPE_DIM:], pos[:, None])
    c_kv = rms_norm(ckv, kv_norm)
    k_rope = rope(kr, pos)
    shp = (b, s, DIFF_HEADS, 2 * DIFF_HEAD_DIM)
    return q_lat, q_rope, c_kv, k_rope, dq.reshape(shp), dk.reshape(shp), dv.reshape(shp)


def mixer_output(mla_lat, diff_o, w_ukv, subln_g, lam_init, w_o):
    b, s = mla_lat.shape[:2]
    w_uv = w_ukv.reshape(KV_LORA_RANK, MLA_HEADS, MLA_NOPE_DIM + MLA_V_DIM)[..., MLA_NOPE_DIM:]
    mla = jnp.einsum('bshr,rhv->bshv', mla_lat, w_uv).reshape(b, s, MLA_HEADS * MLA_V_DIM)
    diff = (rms_norm(diff_o, subln_g) * (1.0 - lam_init)).reshape(b, s, DIFF_WIDTH)
    return jnp.concatenate([mla, diff], axis=-1) @ w_o


def prompt_attention(q_lat, q_rope, c_kv, k_rope, dq, dk, dv, lam):
    s = q_lat.shape[1]
    k_pos = jnp.arange(s)

    def block(start):
        q_pos = start + jnp.arange(Q_BLOCK)
        sl = lambda a: lax.dynamic_slice_in_dim(a, start, Q_BLOCK, axis=1)
        return (mla_core(sl(q_lat), sl(q_rope), c_kv, k_rope, q_pos, k_pos),
                diff_core(sl(dq), dk, dv, q_pos, k_pos, lam))

    mla, dif = lax.map(block, jnp.arange(s // Q_BLOCK) * Q_BLOCK)
    unblock = lambda o: jnp.moveaxis(o, 0, 1).reshape(o.shape[1], s, *o.shape[3:])
    return unblock(mla), unblock(dif)


def sample_attention(q_lat, q_rope, c_kv, k_rope, dq, dk, dv, lam, layer,
                     pool_lat, pool_kr, pool_dk, pool_dv, page_table):
    t = q_lat.shape[1]
    past = page_table.shape[1] * PAGE_SIZE
    q_pos = past + jnp.arange(t)
    k_pos = jnp.arange(past + t)

    def one(args):
        pt, ql, qr, c, kr, q, k, v = args

        def rows(pool, new):
            old = pool[layer, pt].reshape(past, *pool.shape[3:])
            return jnp.concatenate([old, new], axis=0)[None]

        mla = mla_core(ql[None], qr[None], rows(pool_lat, c), rows(pool_kr, kr), q_pos, k_pos)[0]
        dif = diff_core(q[None], rows(pool_dk, k), rows(pool_dv, v), q_pos, k_pos, lam)[0]
        return mla, dif

    return lax.map(one, (page_table, q_lat, q_rope, c_kv, k_rope, dq, dk, dv))


def mem_kv(mem, g, w_mk, w_mv):
    b, m, _ = mem.shape
    mn = rms_norm(mem, g)
    shp = (b, m, MEM_HEADS, MEM_HEAD_DIM)
    return (mn @ w_mk).reshape(shp), (mn @ w_mv).reshape(shp)


def mem_attend(h, k, v, w_mq, w_mo):
    b, s, _ = h.shape
    q = (h @ w_mq).reshape(b, s, MEM_HEADS, MEM_HEAD_DIM)
    sc = jnp.einsum('bqhd,bkhd->bhqk', q, k).astype(jnp.float32) * MEM_HEAD_DIM ** -0.5
    p = jax.nn.softmax(sc, axis=-1).astype(v.dtype)
    o = jnp.einsum('bhqk,bkhd->bqhd', p, v).reshape(b, s, MEM_HEADS * MEM_HEAD_DIM)
    return o @ w_mo


def moe(h, w_router, b_router, w1, b1, w2, b2):
    b, s, d = h.shape
    t = h.reshape(b * s, d)
    logits = (t @ w_router + b_router).astype(jnp.float32)
    top_v, top_i = lax.top_k(logits, TOP_K)
    gates = jax.nn.softmax(top_v, axis=-1)
    combine = jnp.sum(jax.nn.one_hot(top_i, N_EXPERTS, dtype=jnp.float32) * gates[..., None], axis=1)
    out = jnp.zeros_like(t)
    for e in range(N_EXPERTS):
        u = (t @ w1[e] + b1[e]).reshape(b * s, D_FF, 2)
        glu = jnp.minimum(u[..., 0], SWIGLU_LIMIT)
        lin = jnp.clip(u[..., 1], -SWIGLU_LIMIT, SWIGLU_LIMIT)
        a = glu * jax.nn.sigmoid(SWIGLU_ALPHA * glu) * (lin + 1.0)
        out = out + combine[:, e:e + 1].astype(t.dtype) * (a @ w2[e] + b2[e])
    return out.reshape(b, s, d)


def diff_lambda(lq1, lk1, lq2, lk2, lam_init):
    f = lambda a, c: jnp.exp(jnp.sum(a.astype(jnp.float32) * c.astype(jnp.float32)))
    return f(lq1, lk1) - f(lq2, lk2) + lam_init


def layer_forward(x, pos, attend, mem_k, mem_v, lam_init, lw):
    q_lat, q_rope, c_kv, k_rope, dq, dk, dv = mixer_inputs(
        rms_norm(x, lw['attn_norm']), pos, lw['w_in'], lw['q_norm'], lw['w_uq'], lw['kv_norm'], lw['w_ukv'])
    mla_lat, diff_o = attend(q_lat, q_rope, c_kv, k_rope, dq, dk, dv)
    x = x + mixer_output(mla_lat, diff_o, lw['w_ukv'], lw['subln_norm'], lam_init, lw['w_o'])
    x = x + mem_attend(rms_norm(x, lw['mem_q_norm']), mem_k, mem_v, lw['w_mq'], lw['w_mo'])
    x = x + moe(rms_norm(x, lw['ffn_norm']), lw['w_router'], lw['b_router'],
                lw['w_mlp1'], lw['b_mlp1'], lw['w_mlp2'], lw['b_mlp2'])
    return x, c_kv, k_rope, dk, dv


def setup_inputs(seed: int = 0) -> dict:
    key = jax.random.key(seed)
    keys = iter(jax.random.split(key, 48))

    def nrm(shape, scale=1.0):
        return jax.random.normal(next(keys), shape, jnp.float32) * scale

    def gain(shape):
        return 1.0 + 0.02 * nrm(shape)

    n_pages = PAST_LEN // PAGE_SIZE
    n_used = DEC_BATCH * n_pages
    n_phys = n_used + n_used // 4
    page_table = jax.random.permutation(next(keys), n_phys)[:n_used].reshape(DEC_BATCH, n_pages).astype(jnp.int32)
    L, D, dv2 = DEPTH, D_MODEL, 2 * DIFF_HEAD_DIM
    return {
        'x_prompt': nrm((BATCH, SEQ, D)),
        'x_sample': nrm((DEC_BATCH, DEC_SEQ, D)),
        'cache_mla_latent': nrm((L, n_phys, PAGE_SIZE, KV_LORA_RANK)),
        'cache_mla_krope': nrm((L, n_phys, PAGE_SIZE, MLA_ROPE_DIM)),
        'cache_diff_k': nrm((L, n_phys, PAGE_SIZE, DIFF_HEADS, dv2)),
        'cache_diff_v': nrm((L, n_phys, PAGE_SIZE, DIFF_HEADS, dv2)),
        'cache_mem_k': nrm((L, DEC_BATCH, MEM_LEN, MEM_HEADS, MEM_HEAD_DIM)),
        'cache_mem_v': nrm((L, DEC_BATCH, MEM_LEN, MEM_HEADS, MEM_HEAD_DIM)),
        'page_table': page_table,
        'mem_prompt': nrm((BATCH, MEM_LEN, D)),
        'attn_norm': gain((L, D)),
        'w_in': nrm((L, D, IN_COLS), D ** -0.5),
        'q_norm': gain((L, Q_LORA_RANK)),
        'w_uq': nrm((L, Q_LORA_RANK, MLA_HEADS * (MLA_NOPE_DIM + MLA_ROPE_DIM)), Q_LORA_RANK ** -0.5),
        'kv_norm': gain((L, KV_LORA_RANK)),
        'w_ukv': nrm((L, KV_LORA_RANK, MLA_HEADS * (MLA_NOPE_DIM + MLA_V_DIM)), KV_LORA_RANK ** -0.5),
        'lambda_q1': nrm((L, DIFF_HEAD_DIM), 0.1),
        'lambda_k1': nrm((L, DIFF_HEAD_DIM), 0.1),
        'lambda_q2': nrm((L, DIFF_HEAD_DIM), 0.1),
        'lambda_k2': nrm((L, DIFF_HEAD_DIM), 0.1),
        'subln_norm': gain((L, dv2)),
        'w_o': nrm((L, MIX_WIDTH, D), MIX_WIDTH ** -0.5),
        'mem_q_norm': gain((L, D)),
        'mem_kv_norm': gain((L, D)),
        'w_mq': nrm((L, D, MEM_HEADS * MEM_HEAD_DIM), D ** -0.5),
        'w_mk': nrm((L, D, MEM_HEADS * MEM_HEAD_DIM), D ** -0.5),
        'w_mv': nrm((L, D, MEM_HEADS * MEM_HEAD_DIM), D ** -0.5),
        'w_mo': nrm((L, MEM_HEADS * MEM_HEAD_DIM, D), D ** -0.5),
        'ffn_norm': gain((L, D)),
        'w_router': nrm((L, D, N_EXPERTS), D ** -0.5),
        'b_router': nrm((L, N_EXPERTS), 0.01),
        'w_mlp1': nrm((L, N_EXPERTS, D, 2 * D_FF), D ** -0.5),
        'b_mlp1': nrm((L, N_EXPERTS, 2 * D_FF), 0.01),
        'w_mlp2': nrm((L, N_EXPERTS, D_FF, D), D_FF ** -0.5),
        'b_mlp2': nrm((L, N_EXPERTS, D), 0.01),
        'final_norm': gain((D,)),
    }


def reference(x_prompt, x_sample, cache_mla_latent, cache_mla_krope, cache_diff_k, cache_diff_v,
              cache_mem_k, cache_mem_v, page_table, mem_prompt,
              attn_norm, w_in, q_norm, w_uq, kv_norm, w_ukv,
              lambda_q1, lambda_k1, lambda_q2, lambda_k2, subln_norm, w_o,
              mem_q_norm, mem_kv_norm, w_mq, w_mk, w_mv, w_mo,
              ffn_norm, w_router, b_router, w_mlp1, b_mlp1, w_mlp2, b_mlp2, final_norm):
    pos_p = jnp.arange(x_prompt.shape[1])
    pos_s = PAST_LEN + jnp.arange(x_sample.shape[1])
    xp, xs = x_prompt, x_sample
    p_lat, p_kr, p_dk, p_dv, p_mk, p_mv = [], [], [], [], [], []
    s_lat, s_kr, s_dk, s_dv = [], [], [], []
    for l in range(DEPTH):
        lam_init = 0.8 - 0.6 * math.exp(-0.3 * l)
        lam = diff_lambda(lambda_q1[l], lambda_k1[l], lambda_q2[l], lambda_k2[l], lam_init)
        lw = dict(attn_norm=attn_norm[l], w_in=w_in[l], q_norm=q_norm[l], w_uq=w_uq[l],
                  kv_norm=kv_norm[l], w_ukv=w_ukv[l], subln_norm=subln_norm[l], w_o=w_o[l],
                  mem_q_norm=mem_q_norm[l], w_mq=w_mq[l], w_mo=w_mo[l], ffn_norm=ffn_norm[l],
                  w_router=w_router[l], b_router=b_router[l], w_mlp1=w_mlp1[l], b_mlp1=b_mlp1[l],
                  w_mlp2=w_mlp2[l], b_mlp2=b_mlp2[l])
        mk, mv = mem_kv(mem_prompt, mem_kv_norm[l], w_mk[l], w_mv[l])
        xp, c, kr, dk, dv = layer_forward(xp, pos_p, functools.partial(prompt_attention, lam=lam),
                                          mk, mv, lam_init, lw)
        p_lat.append(c); p_kr.append(kr); p_dk.append(dk); p_dv.append(dv); p_mk.append(mk); p_mv.append(mv)
        attend_s = functools.partial(sample_attention, lam=lam, layer=l, pool_lat=cache_mla_latent,
                                     pool_kr=cache_mla_krope, pool_dk=cache_diff_k, pool_dv=cache_diff_v,
                                     page_table=page_table)
        xs, c, kr, dk, dv = layer_forward(xs, pos_s, attend_s, cache_mem_k[l], cache_mem_v[l], lam_init, lw)
        s_lat.append(c); s_kr.append(kr); s_dk.append(dk); s_dv.append(dv)
    y_prompt = rms_norm(xp, final_norm)
    y_sample = rms_norm(xs, final_norm)
    return (y_prompt, y_sample,
            jnp.stack(p_lat), jnp.stack(p_kr), jnp.stack(p_dk), jnp.stack(p_dv), jnp.stack(p_mk), jnp.stack(p_mv),
            jnp.stack(s_lat), jnp.stack(s_kr), jnp.stack(s_dk), jnp.stack(s_dv))
```

```python
import functools
import math

import numpy as np
import jax
import jax.numpy as jnp
from jax import lax
from jax.experimental import pallas as pl
from jax.experimental.pallas import tpu as pltpu

D_MODEL = 1024
MLA_HEADS = 4
MLA_NOPE_DIM = 128
MLA_ROPE_DIM = 64
MLA_V_DIM = 128
Q_LORA_RANK = 384
KV_LORA_RANK = 256
ROPE_BASE = 10000.0
MLA_SCALE = (MLA_NOPE_DIM + MLA_ROPE_DIM) ** -0.5
DIFF_HEADS = 4
DIFF_HEAD_DIM = 64
DIFF_SCALE = DIFF_HEAD_DIM ** -0.5
DIFF_WIDTH = DIFF_HEADS * 2 * DIFF_HEAD_DIM
MEM_HEADS = 4
MEM_HEAD_DIM = D_MODEL // MEM_HEADS
N_EXPERTS = 32
TOP_K = 4
D_FF = D_MODEL
SWIGLU_ALPHA = 1.702
SWIGLU_LIMIT = 7.0
NORM_EPS = 1e-6
NEG_INF = -1e30

LANES = 128
QCAT = KV_LORA_RANK + LANES
DAUG = 2 * LANES
POS_RADIX = 64
VMEM_LIMIT = 56 * 1024 * 1024

BF16 = jnp.bfloat16
F32 = jnp.float32


def _cparams(sem):
    return pltpu.CompilerParams(dimension_semantics=sem, vmem_limit_bytes=VMEM_LIMIT)


def _rms(x, g):
    return x * lax.rsqrt(jnp.mean(x * x, axis=-1, keepdims=True) + NORM_EPS) * g


def _dot(a, b):
    return jnp.dot(a, b, preferred_element_type=F32)


def _dot_nt(a, b):
    return lax.dot_general(a, b, (((1,), (1,)), ((), ())), preferred_element_type=F32)


def _alibi_slope(h):
    return 2.0 ** (-8.0 * (h + 1.0) / DIFF_HEADS)


def _lambda(lam_ref, lam_init):
    v = lam_ref[...]
    a = jnp.sum(v[0:1] * v[1:2], axis=-1, keepdims=True)
    b = jnp.sum(v[2:3] * v[3:4], axis=-1, keepdims=True)
    return jnp.exp(a) - jnp.exp(b) + lam_init


def _proj_kernel(x_ref, cos_ref, sin_ref, aq_ref, ak_ref, ga_ref, wa_ref, gq_ref, wq_ref, gkv_ref, wuk_ref,
                 ckv_ref, kr_ref, dk_ref, dv_ref, qcat_ref, kvcat_ref, dqa_ref, dk1_ref, dk2_ref, dv16_ref):
    dt = qcat_ref.dtype
    h = _rms(x_ref[...], ga_ref[...])
    u = _dot(h.astype(BF16), wa_ref[...])
    o = 0
    cq = u[:, o:o + Q_LORA_RANK]; o += Q_LORA_RANK
    ckv = u[:, o:o + KV_LORA_RANK]; o += KV_LORA_RANK
    krp = u[:, o:o + LANES]; o += LANES
    krs = u[:, o:o + LANES]; o += LANES
    dq = u[:, o:o + DIFF_WIDTH]; o += DIFF_WIDTH
    dk = u[:, o:o + DIFF_WIDTH]; o += DIFF_WIDTH
    dv = u[:, o:o + DIFF_WIDTH]
    cos = cos_ref[...]
    sin = sin_ref[...]
    k_rope = krp * cos + krs * sin
    c_kv = _rms(ckv, gkv_ref[...])
    ckv_ref[...] = c_kv
    kr_ref[...] = k_rope[:, :MLA_ROPE_DIM]
    dk_ref[...] = dk
    dv_ref[...] = dv
    dv16_ref[...] = dv.astype(dt)
    kvcat_ref[:, :KV_LORA_RANK] = c_kv.astype(dt)
    kvcat_ref[:, KV_LORA_RANK:] = k_rope.astype(dt)
    q = _dot(_rms(cq, gq_ref[...]).astype(BF16), wq_ref[...])
    nq = MLA_HEADS * LANES
    lane = lax.broadcasted_iota(jnp.int32, (x_ref.shape[0], LANES), 1)
    ak = ak_ref[...]
    for hh in range(MLA_HEADS):
        sl = slice(hh * LANES, (hh + 1) * LANES)
        q_lat = _dot(q[:, sl].astype(BF16), wuk_ref[hh]) * MLA_SCALE
        q_rope = (q[:, nq + hh * LANES:nq + (hh + 1) * LANES] * cos
                  + q[:, 2 * nq + hh * LANES:2 * nq + (hh + 1) * LANES] * sin) * MLA_SCALE
        qcat_ref[hh, :, :KV_LORA_RANK] = q_lat.astype(dt)
        qcat_ref[hh, :, KV_LORA_RANK:] = q_rope.astype(dt)
    for hh in range(DIFF_HEADS):
        sl = slice(hh * LANES, (hh + 1) * LANES)
        base = hh * DAUG
        dqa_ref[:, base:base + LANES] = (dq[:, sl] * DIFF_SCALE).astype(dt)
        dqa_ref[:, base + LANES:base + DAUG] = aq_ref[:, sl].astype(dt)
        kh = dk[:, sl]
        dk1_ref[:, base:base + LANES] = jnp.where(lane < DIFF_HEAD_DIM, kh, 0.0).astype(dt)
        dk1_ref[:, base + LANES:base + DAUG] = ak.astype(dt)
        dk2_ref[:, base:base + LANES] = jnp.where(lane >= DIFF_HEAD_DIM, kh, 0.0).astype(dt)
        dk2_ref[:, base + LANES:base + DAUG] = ak.astype(dt)


def _proj_weights(w_in, w_uq, w_ukv):
    d = w_in.shape[0]
    half = MLA_ROPE_DIM // 2
    edges = np.cumsum([0, Q_LORA_RANK, KV_LORA_RANK, MLA_ROPE_DIM, DIFF_WIDTH, DIFF_WIDTH, DIFF_WIDTH])
    cq, ckv, kr, dq, dk, dv = [w_in[:, edges[i]:edges[i + 1]] for i in range(6)]
    pad = jnp.zeros((d, LANES - MLA_ROPE_DIM), w_in.dtype)
    kr_sw = jnp.concatenate([kr[:, half:], kr[:, :half]], axis=1)
    wa = jnp.concatenate([cq, ckv, kr, pad, kr_sw, pad, dq, dk, dv], axis=1).astype(BF16)
    wq3 = w_uq.reshape(Q_LORA_RANK, MLA_HEADS, MLA_NOPE_DIM + MLA_ROPE_DIM)
    nope = wq3[:, :, :MLA_NOPE_DIM].reshape(Q_LORA_RANK, MLA_HEADS * MLA_NOPE_DIM)
    rp = wq3[:, :, MLA_NOPE_DIM:]
    pad3 = jnp.zeros((Q_LORA_RANK, MLA_HEADS, LANES - MLA_ROPE_DIM), w_uq.dtype)
    rope = jnp.concatenate([rp, pad3], axis=2).reshape(Q_LORA_RANK, MLA_HEADS * LANES)
    rope_sw = jnp.concatenate([rp[:, :, half:], rp[:, :, :half], pad3], axis=2).reshape(Q_LORA_RANK, MLA_HEADS * LANES)
    wq = jnp.concatenate([nope, rope, rope_sw], axis=1).astype(BF16)
    wkv3 = w_ukv.reshape(KV_LORA_RANK, MLA_HEADS, MLA_NOPE_DIM + MLA_V_DIM)
    wuk = jnp.transpose(wkv3[:, :, :MLA_NOPE_DIM], (1, 2, 0)).astype(BF16)
    wuv = jnp.transpose(wkv3[:, :, MLA_NOPE_DIM:], (1, 0, 2)).astype(BF16)
    return wa, wq, wuk, wuv


def _position_tables(pos):
    half = MLA_ROPE_DIM // 2
    inv = ROPE_BASE ** (-jnp.arange(half, dtype=F32) / half)
    ang = pos.astype(F32)[:, None] * inv
    cos, sin = jnp.cos(ang), jnp.sin(ang)
    z = jnp.zeros((pos.shape[0], LANES - MLA_ROPE_DIM), F32)
    cos2 = jnp.concatenate([cos, cos, z], axis=1)
    sin2 = jnp.concatenate([-sin, sin, z], axis=1)
    hi = (pos // POS_RADIX).astype(F32)[:, None]
    lo = (pos % POS_RADIX).astype(F32)[:, None]
    one = jnp.ones_like(hi)
    zc = jnp.zeros((pos.shape[0], LANES - 4), F32)
    ak = jnp.concatenate([hi, lo, one, one, zc], axis=1)
    aqs = []
    for h in range(DIFF_HEADS):
        s = _alibi_slope(h)
        aqs.append(jnp.concatenate([POS_RADIX * s * one, s * one, -POS_RADIX * s * hi, -s * lo, zc], axis=1))
    return cos2, sin2, jnp.concatenate(aqs, axis=1), ak


def _proj(x, pos, n_rep, attn_norm, wa, q_norm, wq, kv_norm, wuk, dt, tm):
    t, d = x.shape
    s = pos.shape[0]
    assert t == n_rep * s and s % tm == 0
    nsb = s // tm
    cos2, sin2, aq, ak = _position_tables(pos)
    row = lambda i: (i, 0)
    prow = lambda i: (i % nsb, 0)
    full = lambda i: (0, 0)
    ncol = wa.shape[1]
    outs = [
        jax.ShapeDtypeStruct((t, KV_LORA_RANK), F32),
        jax.ShapeDtypeStruct((t, MLA_ROPE_DIM), F32),
        jax.ShapeDtypeStruct((t, DIFF_WIDTH), F32),
        jax.ShapeDtypeStruct((t, DIFF_WIDTH), F32),
        jax.ShapeDtypeStruct((MLA_HEADS, t, QCAT), dt),
        jax.ShapeDtypeStruct((t, QCAT), dt),
        jax.ShapeDtypeStruct((t, DIFF_HEADS * DAUG), dt),
        jax.ShapeDtypeStruct((t, DIFF_HEADS * DAUG), dt),
        jax.ShapeDtypeStruct((t, DIFF_HEADS * DAUG), dt),
        jax.ShapeDtypeStruct((t, DIFF_WIDTH), dt),
    ]
    return pl.pallas_call(
        _proj_kernel,
        grid=(t // tm,),
        in_specs=[
            pl.BlockSpec((tm, d), row),
            pl.BlockSpec((tm, LANES), prow),
            pl.BlockSpec((tm, LANES), prow),
            pl.BlockSpec((tm, DIFF_HEADS * LANES), prow),
            pl.BlockSpec((tm, LANES), prow),
            pl.BlockSpec((1, d), full),
            pl.BlockSpec((d, ncol), full),
            pl.BlockSpec((1, Q_LORA_RANK), full),
            pl.BlockSpec(wq.shape, full),
            pl.BlockSpec((1, KV_LORA_RANK), full),
            pl.BlockSpec(wuk.shape, lambda i: (0, 0, 0)),
        ],
        out_specs=[
            pl.BlockSpec((tm, KV_LORA_RANK), row),
            pl.BlockSpec((tm, MLA_ROPE_DIM), row),
            pl.BlockSpec((tm, DIFF_WIDTH), row),
            pl.BlockSpec((tm, DIFF_WIDTH), row),
            pl.BlockSpec((MLA_HEADS, tm, QCAT), lambda i: (0, i, 0)),
            pl.BlockSpec((tm, QCAT), row),
            pl.BlockSpec((tm, DIFF_HEADS * DAUG), row),
            pl.BlockSpec((tm, DIFF_HEADS * DAUG), row),
            pl.BlockSpec((tm, DIFF_HEADS * DAUG), row),
            pl.BlockSpec((tm, DIFF_WIDTH), row),
        ],
        out_shape=outs,
        compiler_params=_cparams(("parallel",)),
        name="proj",
    )(x, cos2, sin2, aq, ak, attn_norm.reshape(1, d), wa, q_norm.reshape(1, -1), wq, kv_norm.reshape(1, -1), wuk)


def _online_update(s, v, m_ref, l_ref, acc_ref):
    m_prev = m_ref[...]
    m_new = jnp.maximum(m_prev, jnp.max(s, axis=-1, keepdims=True))
    alpha = jnp.exp(m_prev - m_new)
    p = jnp.exp(s - m_new)
    l_ref[...] = alpha * l_ref[...] + jnp.sum(p, axis=-1, keepdims=True)
    acc_ref[...] = alpha * acc_ref[...] + _dot(p.astype(BF16), v)
    m_ref[...] = m_new


def _prompt_attn_kernel(qcat_ref, kvcat_ref, dqa_ref, dk1_ref, dk2_ref, dv_ref, lam_ref,
                        mla_ref, dif_ref,
                        m_mla, l_mla, acc_mla, m_dif, l_dif, acc_dif, *, lam_init):
    qi = pl.program_id(1)
    ki = pl.program_id(2)
    tq = dqa_ref.shape[0]
    tk = kvcat_ref.shape[0]

    @pl.when(ki == 0)
    def _():
        m_mla[...] = jnp.full(m_mla.shape, NEG_INF, F32)
        l_mla[...] = jnp.zeros(l_mla.shape, F32)
        acc_mla[...] = jnp.zeros(acc_mla.shape, F32)
        m_dif[...] = jnp.full(m_dif.shape, NEG_INF, F32)
        l_dif[...] = jnp.zeros(l_dif.shape, F32)
        acc_dif[...] = jnp.zeros(acc_dif.shape, F32)

    def step(masked):
        if masked:
            r = lax.broadcasted_iota(jnp.int32, (tq, tk), 0)
            c = lax.broadcasted_iota(jnp.int32, (tq, tk), 1)
            keep = c <= r
        kv = kvcat_ref[...]
        q = qcat_ref[...].reshape(MLA_HEADS * tq, QCAT)
        s = _dot_nt(q, kv)
        if masked:
            s = jnp.where(keep[None], s.reshape(MLA_HEADS, tq, tk), NEG_INF).reshape(MLA_HEADS * tq, tk)
        _online_update(s, kv[:, :KV_LORA_RANK], m_mla, l_mla, acc_mla)
        for hh in range(DIFF_HEADS):
            qa = dqa_ref[:, hh * DAUG:(hh + 1) * DAUG]
            v = dv_ref[:, hh * LANES:(hh + 1) * LANES]
            for mm, kref in enumerate((dk1_ref, dk2_ref)):
                s = _dot_nt(qa, kref[:, hh * DAUG:(hh + 1) * DAUG])
                if masked:
                    s = jnp.where(keep, s, NEG_INF)
                i = 2 * hh + mm
                _online_update(s, v, m_dif.at[i], l_dif.at[i], acc_dif.at[i])

    @pl.when(ki < qi)
    def _():
        step(False)

    @pl.when(ki == qi)
    def _():
        step(True)
        lam = _lambda(lam_ref, lam_init)
        for hh in range(MLA_HEADS):
            rows = slice(hh * tq, (hh + 1) * tq)
            o = acc_mla[rows, :] * (1.0 / l_mla[rows, :])
            mla_ref[:, hh * KV_LORA_RANK:(hh + 1) * KV_LORA_RANK] = o.astype(mla_ref.dtype)
        for hh in range(DIFF_HEADS):
            o1 = acc_dif[2 * hh] * (1.0 / l_dif[2 * hh])
            o2 = acc_dif[2 * hh + 1] * (1.0 / l_dif[2 * hh + 1])
            dif_ref[:, hh * LANES:(hh + 1) * LANES] = o1 - lam * o2


def _prompt_attn(qcat, kvcat, dqa, dk1, dk2, dv16, lam_rows, lam_init, batch, seq, tq):
    nq = seq // tq
    qmap = lambda b, qi, ki: (b * nq + qi, 0)
    kmap = lambda b, qi, ki: (b * nq + jnp.minimum(ki, qi), 0)
    t = batch * seq
    return pl.pallas_call(
        functools.partial(_prompt_attn_kernel, lam_init=lam_init),
        grid=(batch, nq, nq),
        in_specs=[
            pl.BlockSpec((MLA_HEADS, tq, QCAT), lambda b, qi, ki: (0, b * nq + qi, 0)),
            pl.BlockSpec((tq, QCAT), kmap),
            pl.BlockSpec((tq, DIFF_HEADS * DAUG), qmap),
            pl.BlockSpec((tq, DIFF_HEADS * DAUG), kmap),
            pl.BlockSpec((tq, DIFF_HEADS * DAUG), kmap),
            pl.BlockSpec((tq, DIFF_WIDTH), kmap),
            pl.BlockSpec((4, DIFF_HEAD_DIM), lambda b, qi, ki: (0, 0)),
        ],
        out_specs=[
            pl.BlockSpec((tq, MLA_HEADS * KV_LORA_RANK), qmap),
            pl.BlockSpec((tq, DIFF_WIDTH), qmap),
        ],
        out_shape=[
            jax.ShapeDtypeStruct((t, MLA_HEADS * KV_LORA_RANK), BF16),
            jax.ShapeDtypeStruct((t, DIFF_WIDTH), F32),
        ],
        scratch_shapes=[
            pltpu.VMEM((MLA_HEADS * tq, 1), F32),
            pltpu.VMEM((MLA_HEADS * tq, 1), F32),
            pltpu.VMEM((MLA_HEADS * tq, KV_LORA_RANK), F32),
            pltpu.VMEM((2 * DIFF_HEADS, tq, 1), F32),
            pltpu.VMEM((2 * DIFF_HEADS, tq, 1), F32),
            pltpu.VMEM((2 * DIFF_HEADS, tq, LANES), F32),
        ],
        compiler_params=_cparams(("parallel", "parallel", "arbitrary")),
        name="prompt_attn",
    )(qcat, kvcat, dqa, dk1, dk2, dv16, lam_rows)


def _sample_attn_kernel(pt_ref, qcat_ref, dqa_ref, nlat_ref, nkr_ref, ndk_ref, ndv_ref, lam_ref,
                        lat_hbm, kr_hbm, dk_hbm, dv_hbm,
                        mla_ref, dif_ref,
                        lat_buf, kr_buf, dk_buf, dv_buf, sems, qbd,
                        m_mla, l_mla, acc_mla, m_dif, l_dif, acc_dif,
                        *, lam_init, n_pages, chunk_pages, page, past):
    b = pl.program_id(0)
    c = pl.program_id(1)
    nb = pl.num_programs(0)
    nc = pl.num_programs(1)
    g = b * nc + c
    slot = g % 2
    dt = dqa_ref.shape[0]
    ck = chunk_pages * page
    rows_mla = MLA_HEADS * dt
    rows_dif = 2 * DIFF_HEADS * dt

    def copies(gi, sl):
        base = gi * chunk_pages
        out = []
        for j in range(chunk_pages):
            pid = pt_ref[base + j]
            out.append(pltpu.make_async_copy(lat_hbm.at[pid], lat_buf.at[sl, j], sems.at[sl, 0]))
            out.append(pltpu.make_async_copy(kr_hbm.at[pid], kr_buf.at[sl, j], sems.at[sl, 1]))
            out.append(pltpu.make_async_copy(dk_hbm.at[pid], dk_buf.at[sl, j], sems.at[sl, 2]))
            out.append(pltpu.make_async_copy(dv_hbm.at[pid], dv_buf.at[sl, j], sems.at[sl, 3]))
        return out

    @pl.when(g == 0)
    def _():
        for cp in copies(g, slot):
            cp.start()

    @pl.when(g + 1 < nb * nc)
    def _():
        for cp in copies(g + 1, 1 - slot):
            cp.start()

    row = lax.broadcasted_iota(jnp.int32, (rows_dif, 1), 0)
    tok_d = row % dt
    head_d = (row // dt) % DIFF_HEADS
    slope = jnp.exp2(-8.0 * (head_d.astype(F32) + 1.0) / DIFF_HEADS)

    @pl.when(c == 0)
    def _():
        m_mla[...] = jnp.full(m_mla.shape, NEG_INF, F32)
        l_mla[...] = jnp.zeros(l_mla.shape, F32)
        acc_mla[...] = jnp.zeros(acc_mla.shape, F32)
        m_dif[...] = jnp.full(m_dif.shape, NEG_INF, F32)
        l_dif[...] = jnp.zeros(l_dif.shape, F32)
        acc_dif[...] = jnp.zeros(acc_dif.shape, F32)
        dq = jnp.concatenate([dqa_ref[:, hh * DAUG:hh * DAUG + LANES] for hh in range(DIFF_HEADS)], axis=1)
        rep = jnp.broadcast_to(dq[None], (2 * DIFF_HEADS, dt, DIFF_WIDTH)).reshape(rows_dif, DIFF_WIDTH)
        col = lax.broadcasted_iota(jnp.int32, (rows_dif, DIFF_WIDTH), 1)
        own = (col // LANES == head_d) & ((col % LANES) // DIFF_HEAD_DIM == row // (DIFF_HEADS * dt))
        qbd[...] = jnp.where(own, rep, 0.0).astype(BF16)

    for cp in copies(g, slot):
        cp.wait()

    q = qcat_ref[...].reshape(rows_mla, QCAT)
    q_lat = q[:, :KV_LORA_RANK].astype(BF16)
    q_rope = q[:, KV_LORA_RANK:KV_LORA_RANK + MLA_ROPE_DIM].astype(BF16)

    lat = lat_buf[slot].reshape(ck, KV_LORA_RANK).astype(BF16)
    kr = kr_buf[slot].reshape(ck, MLA_ROPE_DIM).astype(BF16)
    s = _dot_nt(q_lat, lat) + _dot_nt(q_rope, kr)
    _online_update(s, lat, m_mla, l_mla, acc_mla)

    kd = dk_buf[slot].reshape(ck, DIFF_WIDTH).astype(BF16)
    vd = dv_buf[slot].reshape(ck, DIFF_WIDTH).astype(BF16)
    kcol = lax.broadcasted_iota(jnp.int32, (rows_dif, ck), 1)
    dist = (past + tok_d - c * ck) - kcol
    s = _dot_nt(qbd[...], kd) - slope * dist.astype(F32)
    _online_update(s, vd, m_dif, l_dif, acc_dif)

    @pl.when(c == nc - 1)
    def _():
        nlat = nlat_ref[...].astype(BF16)
        kcol_m = lax.broadcasted_iota(jnp.int32, (rows_mla, dt), 1)
        tok_m = lax.broadcasted_iota(jnp.int32, (rows_mla, dt), 0) % dt
        s = _dot_nt(q_lat, nlat) + _dot_nt(q_rope, nkr_ref[...].astype(BF16))
        s = jnp.where(kcol_m <= tok_m, s, NEG_INF)
        _online_update(s, nlat, m_mla, l_mla, acc_mla)
        kcol_n = lax.broadcasted_iota(jnp.int32, (rows_dif, dt), 1)
        s = _dot_nt(qbd[...], ndk_ref[...].astype(BF16)) - slope * (tok_d - kcol_n).astype(F32)
        s = jnp.where(kcol_n <= tok_d, s, NEG_INF)
        _online_update(s, ndv_ref[...].astype(BF16), m_dif, l_dif, acc_dif)

        lam = _lambda(lam_ref, lam_init)
        for hh in range(MLA_HEADS):
            rows = slice(hh * dt, (hh + 1) * dt)
            mla_ref[:, hh * KV_LORA_RANK:(hh + 1) * KV_LORA_RANK] = acc_mla[rows, :] * (1.0 / l_mla[rows, :])
        half = DIFF_HEADS * dt
        for hh in range(DIFF_HEADS):
            r1 = slice(hh * dt, (hh + 1) * dt)
            r2 = slice(half + hh * dt, half + (hh + 1) * dt)
            cols = slice(hh * LANES, (hh + 1) * LANES)
            o1 = acc_dif[r1, cols] * (1.0 / l_dif[r1, :])
            o2 = acc_dif[r2, cols] * (1.0 / l_dif[r2, :])
            dif_ref[:, cols] = o1 - lam * o2


def _sample_attn(page_table, qcat, dqa, nlat, nkr, ndk, ndv, lam_rows, pool_lat, pool_kr, pool_dk, pool_dv,
                 lam_init, nb, dt, chunk_pages):
    n_pages = page_table.shape[1]
    page = pool_lat.shape[1]
    assert n_pages % chunk_pages == 0
    nc = n_pages // chunk_pages
    past = n_pages * page
    t = nb * dt
    tok = lambda b, c, pt: (b, 0)
    any_spec = pl.BlockSpec(memory_space=pl.ANY)
    kern = functools.partial(_sample_attn_kernel, lam_init=lam_init, n_pages=n_pages,
                             chunk_pages=chunk_pages, page=page, past=past)
    grid_spec = pltpu.PrefetchScalarGridSpec(
        num_scalar_prefetch=1,
        grid=(nb, nc),
        in_specs=[
            pl.BlockSpec((MLA_HEADS, dt, QCAT), lambda b, c, pt: (0, b, 0)),
            pl.BlockSpec((dt, DIFF_HEADS * DAUG), tok),
            pl.BlockSpec((dt, KV_LORA_RANK), tok),
            pl.BlockSpec((dt, MLA_ROPE_DIM), tok),
            pl.BlockSpec((dt, DIFF_WIDTH), tok),
            pl.BlockSpec((dt, DIFF_WIDTH), tok),
            pl.BlockSpec((4, DIFF_HEAD_DIM), lambda b, c, pt: (0, 0)),
            any_spec, any_spec, any_spec, any_spec,
        ],
        out_specs=[
            pl.BlockSpec((dt, MLA_HEADS * KV_LORA_RANK), tok),
            pl.BlockSpec((dt, DIFF_WIDTH), tok),
        ],
        scratch_shapes=[
            pltpu.VMEM((2, chunk_pages, page, KV_LORA_RANK), F32),
            pltpu.VMEM((2, chunk_pages, page, MLA_ROPE_DIM), F32),
            pltpu.VMEM((2, chunk_pages, page, DIFF_WIDTH), F32),
            pltpu.VMEM((2, chunk_pages, page, DIFF_WIDTH), F32),
            pltpu.SemaphoreType.DMA((2, 4)),
            pltpu.VMEM((2 * DIFF_HEADS * dt, DIFF_WIDTH), BF16),
            pltpu.VMEM((MLA_HEADS * dt, 1), F32),
            pltpu.VMEM((MLA_HEADS * dt, 1), F32),
            pltpu.VMEM((MLA_HEADS * dt, KV_LORA_RANK), F32),
            pltpu.VMEM((2 * DIFF_HEADS * dt, 1), F32),
            pltpu.VMEM((2 * DIFF_HEADS * dt, 1), F32),
            pltpu.VMEM((2 * DIFF_HEADS * dt, DIFF_WIDTH), F32),
        ],
    )
    return pl.pallas_call(
        kern,
        grid_spec=grid_spec,
        out_shape=[
            jax.ShapeDtypeStruct((t, MLA_HEADS * KV_LORA_RANK), F32),
            jax.ShapeDtypeStruct((t, DIFF_WIDTH), F32),
        ],
        compiler_params=_cparams(("arbitrary", "arbitrary")),
        name="sample_attn",
    )(page_table.reshape(-1), qcat, dqa, nlat, nkr, ndk, ndv, lam_rows, pool_lat, pool_kr, pool_dk, pool_dv)


def _mix_out_kernel(lat_ref, dif_ref, x_ref, wuv_ref, gs_ref, wo_ref, gm_ref, wmq_ref,
                    x1_ref, qm_ref, *, lam_init):
    parts = []
    for hh in range(MLA_HEADS):
        lat = lat_ref[:, hh * KV_LORA_RANK:(hh + 1) * KV_LORA_RANK].astype(BF16)
        parts.append(_dot(lat, wuv_ref[hh]))
    gs = gs_ref[...]
    for hh in range(DIFF_HEADS):
        parts.append(_rms(dif_ref[:, hh * LANES:(hh + 1) * LANES], gs) * (1.0 - lam_init))
    mix = jnp.concatenate(parts, axis=1).astype(BF16)
    x1 = x_ref[...] + _dot(mix, wo_ref[...])
    x1_ref[...] = x1
    hq = _rms(x1, gm_ref[...]).astype(BF16)
    qm_ref[...] = (_dot(hq, wmq_ref[...]) * MEM_HEAD_DIM ** -0.5).astype(qm_ref.dtype)


def _mix_out(mla_lat, diff_o, x, wuv, subln, wo, gm, wmq, lam_init, dt, tm):
    t, d = x.shape
    row = lambda i: (i, 0)
    full = lambda i: (0, 0)
    return pl.pallas_call(
        functools.partial(_mix_out_kernel, lam_init=lam_init),
        grid=(t // tm,),
        in_specs=[
            pl.BlockSpec((tm, MLA_HEADS * KV_LORA_RANK), row),
            pl.BlockSpec((tm, DIFF_WIDTH), row),
            pl.BlockSpec((tm, d), row),
            pl.BlockSpec(wuv.shape, lambda i: (0, 0, 0)),
            pl.BlockSpec((1, LANES), full),
            pl.BlockSpec(wo.shape, full),
            pl.BlockSpec((1, d), full),
            pl.BlockSpec(wmq.shape, full),
        ],
        out_specs=[pl.BlockSpec((tm, d), row), pl.BlockSpec((tm, d), row)],
        out_shape=[jax.ShapeDtypeStruct((t, d), F32), jax.ShapeDtypeStruct((t, d), dt)],
        compiler_params=_cparams(("parallel",)),
        name="mix_out",
    )(mla_lat, diff_o, x, wuv, subln.reshape(1, -1), wo, gm.reshape(1, d), wmq)


def _mem_kv_kernel(mem_ref, g_ref, wk_ref, wv_ref, k_ref, v_ref):
    mn = _rms(mem_ref[...], g_ref[...]).astype(BF16)
    k_ref[...] = _dot(mn, wk_ref[...])
    v_ref[...] = _dot(mn, wv_ref[...])


def _mem_kv(mem, g, wk, wv, tm):
    t, d = mem.shape
    row = lambda i: (i, 0)
    full = lambda i: (0, 0)
    return pl.pallas_call(
        _mem_kv_kernel,
        grid=(t // tm,),
        in_specs=[pl.BlockSpec((tm, d), row), pl.BlockSpec((1, d), full),
                  pl.BlockSpec(wk.shape, full), pl.BlockSpec(wv.shape, full)],
        out_specs=[pl.BlockSpec((tm, d), row), pl.BlockSpec((tm, d), row)],
        out_shape=[jax.ShapeDtypeStruct((t, d), F32), jax.ShapeDtypeStruct((t, d), F32)],
        compiler_params=_cparams(("parallel",)),
        name="mem_kv",
    )(mem, g.reshape(1, d), wk, wv)


def _mem_attn_kernel(q_ref, k_ref, v_ref, o_ref):
    for hh in range(MEM_HEADS):
        cols = slice(hh * MEM_HEAD_DIM, (hh + 1) * MEM_HEAD_DIM)
        q = q_ref[:, cols].astype(BF16)
        k = k_ref[0, :, cols].astype(BF16)
        v = v_ref[0, :, cols].astype(BF16)
        s = _dot_nt(q, k)
        p = jnp.exp(s - jnp.max(s, axis=-1, keepdims=True))
        o = _dot(p.astype(BF16), v) * (1.0 / jnp.sum(p, axis=-1, keepdims=True))
        o_ref[:, cols] = o.astype(o_ref.dtype)


def _mem_attn(qm, mk, mv, nb, rows_per_batch, tq, dt):
    t, d = qm.shape
    nqb = rows_per_batch // tq
    m = mk.shape[1]
    return pl.pallas_call(
        _mem_attn_kernel,
        grid=(nb, nqb),
        in_specs=[
            pl.BlockSpec((tq, d), lambda b, i: (b * nqb + i, 0)),
            pl.BlockSpec((1, m, d), lambda b, i: (b, 0, 0)),
            pl.BlockSpec((1, m, d), lambda b, i: (b, 0, 0)),
        ],
        out_specs=pl.BlockSpec((tq, d), lambda b, i: (b * nqb + i, 0)),
        out_shape=jax.ShapeDtypeStruct((t, d), dt),
        compiler_params=_cparams(("parallel", "parallel")),
        name="mem_attn",
    )(qm, mk, mv)


def _post_attn_kernel(om_ref, x1_ref, wmo_ref, gf_ref, wr_ref, br_ref, cnt_in_ref,
                      x2_ref, hn_ref, idx_ref, gate_ref, rank_ref, cnt_ref, carry):
    tm = om_ref.shape[0]

    @pl.when(pl.program_id(0) == 0)
    def _():
        carry[...] = cnt_in_ref[...]

    x2 = x1_ref[...] + _dot(om_ref[...].astype(BF16), wmo_ref[...])
    x2_ref[...] = x2
    hn = _rms(x2, gf_ref[...])
    hn_ref[...] = hn
    logits = jnp.dot(hn, wr_ref[...], preferred_element_type=F32, precision=lax.Precision.HIGHEST) + br_ref[...]
    lane_e = lax.broadcasted_iota(jnp.int32, (tm, N_EXPERTS), 1)
    lane_o = lax.broadcasted_iota(jnp.int32, (tm, LANES), 1)
    work = logits
    vals, hots = [], []
    idx_out = jnp.zeros((tm, LANES), jnp.int32)
    for k in range(TOP_K):
        mx = jnp.max(work, axis=-1, keepdims=True)
        idx = jnp.min(jnp.where(work == mx, lane_e, N_EXPERTS), axis=-1, keepdims=True)
        hot = lane_e == idx
        vals.append(mx)
        hots.append(hot)
        idx_out = jnp.where(lane_o == k, idx, idx_out)
        work = jnp.where(hot, -jnp.inf, work)
    es = [jnp.exp(v - vals[0]) for v in vals]
    den = es[0] + es[1] + es[2] + es[3]
    gate_out = jnp.zeros((tm, LANES), F32)
    for k in range(TOP_K):
        gate_out = jnp.where(lane_o == k, es[k] * (1.0 / den), gate_out)
    multi = (hots[0] | hots[1] | hots[2] | hots[3]).astype(F32)
    r = lax.broadcasted_iota(jnp.int32, (tm, tm), 0)
    cc = lax.broadcasted_iota(jnp.int32, (tm, tm), 1)
    below = (cc < r).astype(BF16)
    base = carry[...] + _dot(below, multi.astype(BF16))
    rank_out = jnp.zeros((tm, LANES), jnp.int32)
    for k in range(TOP_K):
        rk = jnp.sum(jnp.where(hots[k], base, 0.0), axis=-1, keepdims=True).astype(jnp.int32)
        rank_out = jnp.where(lane_o == k, rk, rank_out)
    idx_ref[...] = idx_out
    gate_ref[...] = gate_out
    rank_ref[...] = rank_out
    new = carry[...] + jnp.sum(multi, axis=0, keepdims=True)
    carry[...] = new
    cnt_ref[...] = new


def _post_attn(om, x1, wmo, gf, wr, br, cnt_in, tm):
    t, d = x1.shape
    row = lambda i: (i, 0)
    full = lambda i: (0, 0)
    return pl.pallas_call(
        _post_attn_kernel,
        grid=(t // tm,),
        in_specs=[
            pl.BlockSpec((tm, d), row), pl.BlockSpec((tm, d), row),
            pl.BlockSpec(wmo.shape, full), pl.BlockSpec((1, d), full),
            pl.BlockSpec(wr.shape, full), pl.BlockSpec((1, N_EXPERTS), full),
            pl.BlockSpec((1, N_EXPERTS), full),
        ],
        out_specs=[
            pl.BlockSpec((tm, d), row), pl.BlockSpec((tm, d), row),
            pl.BlockSpec((tm, LANES), row), pl.BlockSpec((tm, LANES), row), pl.BlockSpec((tm, LANES), row),
            pl.BlockSpec((1, N_EXPERTS), full),
        ],
        out_shape=[
            jax.ShapeDtypeStruct((t, d), F32), jax.ShapeDtypeStruct((t, d), F32),
            jax.ShapeDtypeStruct((t, LANES), jnp.int32), jax.ShapeDtypeStruct((t, LANES), F32),
            jax.ShapeDtypeStruct((t, LANES), jnp.int32),
            jax.ShapeDtypeStruct((1, N_EXPERTS), F32),
        ],
        scratch_shapes=[pltpu.VMEM((1, N_EXPERTS), F32)],
        compiler_params=_cparams(("arbitrary",)),
        name="post_attn",
    )(om, x1, wmo, gf.reshape(1, d), wr, br.reshape(1, N_EXPERTS), cnt_in)


MXU_TILE = 256


def _deinterleave_matrix():
    p = np.zeros((MXU_TILE, MXU_TILE), np.float32)
    half = MXU_TILE // 2
    for i in range(half):
        p[2 * i, i] = 1.0
        p[2 * i + 1, half + i] = 1.0
    return jnp.asarray(p, BF16)


def _moe_prep_kernel(w1_ref, w2_ref, perm_ref, w1g_ref, w1l_ref, w2b_ref):
    half = MXU_TILE // 2
    perm = perm_ref[...]
    for cb in range(w1_ref.shape[2] // MXU_TILE):
        blk = w1_ref[0, :, cb * MXU_TILE:(cb + 1) * MXU_TILE].astype(BF16)
        r = _dot(blk, perm)
        w1g_ref[0, :, cb * half:(cb + 1) * half] = r[:, :half].astype(BF16)
        w1l_ref[0, :, cb * half:(cb + 1) * half] = r[:, half:].astype(BF16)
    w2b_ref[...] = w2_ref[...].astype(BF16)


def _moe_prep(w1, w2):
    e, d, f2 = w1.shape
    f = f2 // 2
    blk = lambda i: (i, 0, 0)
    return pl.pallas_call(
        _moe_prep_kernel,
        grid=(e,),
        in_specs=[pl.BlockSpec((1, d, f2), blk), pl.BlockSpec((1, f, d), blk),
                  pl.BlockSpec((MXU_TILE, MXU_TILE), lambda i: (0, 0))],
        out_specs=[pl.BlockSpec((1, d, f), blk), pl.BlockSpec((1, d, f), blk), pl.BlockSpec((1, f, d), blk)],
        out_shape=[jax.ShapeDtypeStruct((e, d, f), BF16), jax.ShapeDtypeStruct((e, d, f), BF16),
                   jax.ShapeDtypeStruct((e, f, d), BF16)],
        compiler_params=_cparams(("parallel",)),
        name="moe_prep",
    )(w1, w2, _deinterleave_matrix())


def _dispatch_kernel(dest_ref, hn_ref, xs_in_ref, xs_ref, sem):
    del xs_in_ref
    tm = hn_ref.shape[0]
    base = pl.program_id(0) * tm * TOP_K

    def copy(j):
        return pltpu.make_async_copy(hn_ref.at[j // TOP_K], xs_ref.at[dest_ref[base + j]], sem)

    def start(j, carry):
        copy(j).start()
        return carry

    def wait(j, carry):
        copy(j).wait()
        return carry

    lax.fori_loop(0, tm * TOP_K, start, 0)
    lax.fori_loop(0, tm * TOP_K, wait, 0)


def _dispatch(dest_flat, hn, xs, tm):
    t, d = hn.shape
    grid_spec = pltpu.PrefetchScalarGridSpec(
        num_scalar_prefetch=1,
        grid=(t // tm,),
        in_specs=[pl.BlockSpec((tm, d), lambda i, dest: (i, 0)), pl.BlockSpec(memory_space=pl.ANY)],
        out_specs=pl.BlockSpec(memory_space=pl.ANY),
        scratch_shapes=[pltpu.SemaphoreType.DMA(())],
    )
    return pl.pallas_call(
        _dispatch_kernel,
        grid_spec=grid_spec,
        out_shape=jax.ShapeDtypeStruct(xs.shape, xs.dtype),
        input_output_aliases={2: 0},
        compiler_params=pltpu.CompilerParams(dimension_semantics=("arbitrary",), vmem_limit_bytes=VMEM_LIMIT,
                                             has_side_effects=True),
        name="dispatch",
    )(dest_flat, hn, xs)


def _experts_kernel(te_ref, tv_ref, x_ref, w1g_ref, w1l_ref, w2_ref, b1g_ref, b1l_ref, b2_ref, y_ref):
    @pl.when(tv_ref[pl.program_id(0)] > 0)
    def _():
        x = x_ref[...].astype(BF16)
        glu = jnp.minimum(_dot(x, w1g_ref[0]) + b1g_ref[0], SWIGLU_LIMIT)
        lin = jnp.clip(_dot(x, w1l_ref[0]) + b1l_ref[0], -SWIGLU_LIMIT, SWIGLU_LIMIT)
        a = glu * jax.nn.sigmoid(SWIGLU_ALPHA * glu) * (lin + 1.0)
        y_ref[...] = _dot(a.astype(BF16), w2_ref[0]) + b2_ref[0]

    @pl.when(tv_ref[pl.program_id(0)] == 0)
    def _():
        y_ref[...] = jnp.zeros(y_ref.shape, F32)


def _experts(tile_expert, tile_valid, xs, w1g, w1l, w2b, b1g, b1l, b2, tmoe):
    r, d = xs.shape
    f = w1g.shape[2]
    wmap = lambda g, te, tv: (te[g], 0, 0)
    grid_spec = pltpu.PrefetchScalarGridSpec(
        num_scalar_prefetch=2,
        grid=(r // tmoe,),
        in_specs=[
            pl.BlockSpec((tmoe, d), lambda g, te, tv: (g, 0)),
            pl.BlockSpec((1, d, f), wmap), pl.BlockSpec((1, d, f), wmap), pl.BlockSpec((1, f, d), wmap),
            pl.BlockSpec((1, 1, f), wmap), pl.BlockSpec((1, 1, f), wmap), pl.BlockSpec((1, 1, d), wmap),
        ],
        out_specs=pl.BlockSpec((tmoe, d), lambda g, te, tv: (g, 0)),
    )
    return pl.pallas_call(
        _experts_kernel,
        grid_spec=grid_spec,
        out_shape=jax.ShapeDtypeStruct((r, d), F32),
        compiler_params=_cparams(("arbitrary",)),
        name="moe_experts",
    )(tile_expert, tile_valid, xs, w1g, w1l, w2b, b1g, b1l, b2)


def _combine_kernel(dest_ref, x2_ref, gate_ref, gf_ref, ys_hbm, y_ref, buf, sem):
    tm = x2_ref.shape[0]
    base = pl.program_id(0) * tm * TOP_K

    def copy(j):
        return pltpu.make_async_copy(ys_hbm.at[dest_ref[base + j]], buf.at[j % TOP_K, j // TOP_K], sem)

    def start(j, carry):
        copy(j).start()
        return carry

    def wait(j, carry):
        copy(j).wait()
        return carry

    lax.fori_loop(0, tm * TOP_K, start, 0)
    lax.fori_loop(0, tm * TOP_K, wait, 0)
    gate = gate_ref[...]
    out = x2_ref[...]
    for k in range(TOP_K):
        out = out + gate[:, k:k + 1] * buf[k]
    y_ref[...] = _rms(out, gf_ref[...])


def _combine(dest_flat, x2, gate, final_norm, ys, tm):
    t, d = x2.shape
    grid_spec = pltpu.PrefetchScalarGridSpec(
        num_scalar_prefetch=1,
        grid=(t // tm,),
        in_specs=[
            pl.BlockSpec((tm, d), lambda i, dest: (i, 0)),
            pl.BlockSpec((tm, LANES), lambda i, dest: (i, 0)),
            pl.BlockSpec((1, d), lambda i, dest: (0, 0)),
            pl.BlockSpec(memory_space=pl.ANY),
        ],
        out_specs=pl.BlockSpec((tm, d), lambda i, dest: (i, 0)),
        scratch_shapes=[pltpu.VMEM((TOP_K, tm, d), F32), pltpu.SemaphoreType.DMA(())],
    )
    return pl.pallas_call(
        _combine_kernel,
        grid_spec=grid_spec,
        out_shape=jax.ShapeDtypeStruct((t, d), F32),
        compiler_params=_cparams(("arbitrary",)),
        name="combine",
    )(dest_flat, x2, gate, final_norm.reshape(1, d), ys)


def _slot_plan(counts, n_tiles, tmoe):
    counts = counts.reshape(-1).astype(jnp.int32)
    tiles = (counts + tmoe - 1) // tmoe
    ends = jnp.cumsum(tiles)
    row_start = (ends - tiles) * tmoe
    g = jnp.arange(n_tiles, dtype=jnp.int32)
    total = ends[-1]
    last_expert = jnp.max(jnp.where(counts > 0, jnp.arange(N_EXPERTS, dtype=jnp.int32), 0))
    te = jnp.sum((g[:, None] >= ends[None, :]).astype(jnp.int32), axis=1)
    tile_expert = jnp.where(g < total, jnp.minimum(te, N_EXPERTS - 1), last_expert).astype(jnp.int32)
    tile_valid = (g < total).astype(jnp.int32)
    return row_start, tile_expert, tile_valid


TM_TOKENS = 256
TQ_ATTN = 256
TM_ROUTE = 256
TM_EXPERT = 256
CHUNK_PAGES = 8


def kernel(x_prompt, x_sample, cache_mla_latent, cache_mla_krope, cache_diff_k, cache_diff_v, cache_mem_k, cache_mem_v, page_table, mem_prompt, attn_norm, w_in, q_norm, w_uq, kv_norm, w_ukv, lambda_q1, lambda_k1, lambda_q2, lambda_k2, subln_norm, w_o, mem_q_norm, mem_kv_norm, w_mq, w_mk, w_mv, w_mo, ffn_norm, w_router, b_router, w_mlp1, b_mlp1, w_mlp2, b_mlp2, final_norm):
    depth = w_in.shape[0]
    assert depth == 1
    for h in range(DIFF_HEADS):
        assert math.frexp(_alibi_slope(h))[0] == 0.5
    l = 0
    lam_init = 0.8 - 0.6 * math.exp(-0.3 * l)
    nbp, seq, d = x_prompt.shape
    nbs, dseq, _ = x_sample.shape
    n_phys, page = cache_mla_latent.shape[1:3]
    n_pages = page_table.shape[1]
    past = n_pages * page
    mem_len = mem_prompt.shape[1]
    assert seq <= POS_RADIX * POS_RADIX
    tp, ts = nbp * seq, nbs * dseq

    wa, wq, wuk, wuv = _proj_weights(w_in[l], w_uq[l], w_ukv[l])
    lam_rows = jnp.stack([lambda_q1[l], lambda_k1[l], lambda_q2[l], lambda_k2[l]]).astype(F32)
    wo = w_o[l].astype(BF16)
    wmq = w_mq[l].astype(BF16)
    wmo = w_mo[l].astype(BF16)

    xp = x_prompt.reshape(tp, d)
    tm_p = min(TM_TOKENS, seq)
    (p_ckv, p_kr, p_dk, p_dv, p_qcat, p_kvcat, p_dqa, p_dk1, p_dk2, p_dv16) = _proj(
        xp, jnp.arange(seq, dtype=jnp.int32), nbp, attn_norm[l], wa, q_norm[l], wq, kv_norm[l], wuk, BF16, tm_p)
    p_lat, p_dif = _prompt_attn(p_qcat, p_kvcat, p_dqa, p_dk1, p_dk2, p_dv16, lam_rows, lam_init,
                                nbp, seq, min(TQ_ATTN, seq))
    p_x1, p_qm = _mix_out(p_lat, p_dif, xp, wuv, subln_norm[l], wo, mem_q_norm[l], wmq, lam_init, BF16, tm_p)
    mk, mv = _mem_kv(mem_prompt.reshape(nbp * mem_len, d), mem_kv_norm[l], w_mk[l].astype(BF16),
                     w_mv[l].astype(BF16), min(TM_TOKENS, nbp * mem_len))
    p_om = _mem_attn(p_qm, mk.reshape(nbp, mem_len, d), mv.reshape(nbp, mem_len, d), nbp, seq, tm_p, BF16)

    xs_tok = x_sample.reshape(ts, d)
    tm_s = min(TM_TOKENS, ts)
    pos_s = past + jnp.arange(dseq, dtype=jnp.int32)
    (s_ckv, s_kr, s_dk, s_dv, s_qcat, _, s_dqa, _, _, _) = _proj(
        xs_tok, jnp.tile(pos_s, tm_s // dseq), ts // tm_s, attn_norm[l], wa, q_norm[l], wq, kv_norm[l], wuk,
        F32, tm_s)
    s_lat, s_dif = _sample_attn(
        page_table, s_qcat, s_dqa, s_ckv, s_kr, s_dk, s_dv, lam_rows,
        cache_mla_latent.reshape(n_phys, page, KV_LORA_RANK), cache_mla_krope.reshape(n_phys, page, MLA_ROPE_DIM),
        cache_diff_k.reshape(n_phys, page, DIFF_WIDTH), cache_diff_v.reshape(n_phys, page, DIFF_WIDTH),
        lam_init, nbs, dseq, min(CHUNK_PAGES, n_pages))
    s_x1, s_qm = _mix_out(s_lat, s_dif, xs_tok, wuv, subln_norm[l], wo, mem_q_norm[l], wmq, lam_init, F32, tm_s)
    s_om = _mem_attn(s_qm, cache_mem_k.reshape(nbs, mem_len, d), cache_mem_v.reshape(nbs, mem_len, d),
                     nbs, dseq, dseq, F32)

    tr_p, tr_s = min(TM_ROUTE, tp), min(TM_ROUTE, ts)
    zero_cnt = jnp.zeros((1, N_EXPERTS), F32)
    p_x2, p_hn, p_idx, p_gate, p_rank, cnt_p = _post_attn(p_om, p_x1, wmo, ffn_norm[l], w_router[l], b_router[l],
                                                          zero_cnt, tr_p)
    s_x2, s_hn, s_idx, s_gate, s_rank, cnt = _post_attn(s_om, s_x1, wmo, ffn_norm[l], w_router[l], b_router[l],
                                                        cnt_p, tr_s)
    n_pairs = (tp + ts) * TOP_K
    n_tiles = (n_pairs + N_EXPERTS * (TM_EXPERT - 1)) // TM_EXPERT + 1
    row_start, tile_expert, tile_valid = _slot_plan(cnt, n_tiles, TM_EXPERT)
    p_dest = (row_start[p_idx[:, :TOP_K]] + p_rank[:, :TOP_K]).reshape(-1)
    s_dest = (row_start[s_idx[:, :TOP_K]] + s_rank[:, :TOP_K]).reshape(-1)

    w1g, w1l, w2b = _moe_prep(w_mlp1.reshape(w_mlp1.shape[1:]), w_mlp2.reshape(w_mlp2.shape[1:]))
    b1 = b_mlp1[l].reshape(N_EXPERTS, D_FF, 2)
    b1g = b1[:, :, 0].reshape(N_EXPERTS, 1, D_FF)
    b1l = b1[:, :, 1].reshape(N_EXPERTS, 1, D_FF)
    b2 = b_mlp2[l].reshape(N_EXPERTS, 1, d)
    slots = jnp.zeros((n_tiles * TM_EXPERT, d), F32)
    slots = _dispatch(p_dest, p_hn, slots, tr_p)
    slots = _dispatch(s_dest, s_hn, slots, tr_s)
    ys = _experts(tile_expert, tile_valid, slots, w1g, w1l, w2b, b1g, b1l, b2, TM_EXPERT)
    y_p = _combine(p_dest, p_x2, p_gate, final_norm, ys, tr_p)
    y_s = _combine(s_dest, s_x2, s_gate, final_norm, ys, tr_s)

    dshape = (DIFF_HEADS, 2 * DIFF_HEAD_DIM)
    mshape = (MEM_HEADS, MEM_HEAD_DIM)
    return (
        y_p.reshape(nbp, seq, d),
        y_s.reshape(nbs, dseq, d),
        p_ckv.reshape(1, nbp, seq, KV_LORA_RANK),
        p_kr.reshape(1, nbp, seq, MLA_ROPE_DIM),
        p_dk.reshape(1, nbp, seq, *dshape),
        p_dv.reshape(1, nbp, seq, *dshape),
        mk.reshape(1, nbp, mem_len, *mshape),
        mv.reshape(1, nbp, mem_len, *mshape),
        s_ckv.reshape(1, nbs, dseq, KV_LORA_RANK),
        s_kr.reshape(1, nbs, dseq, MLA_ROPE_DIM),
        s_dk.reshape(1, nbs, dseq, *dshape),
        s_dv.reshape(1, nbs, dseq, *dshape),
    )
```

```python
import functools
import math

import numpy as np
import jax
import jax.numpy as jnp
from jax import lax
from jax.experimental import pallas as pl
from jax.experimental.pallas import tpu as pltpu

D_MODEL = 1024
MLA_HEADS = 4
MLA_NOPE_DIM = 128
MLA_ROPE_DIM = 64
MLA_V_DIM = 128
Q_LORA_RANK = 384
KV_LORA_RANK = 256
ROPE_BASE = 10000.0
MLA_SCALE = (MLA_NOPE_DIM + MLA_ROPE_DIM) ** -0.5
DIFF_HEADS = 4
DIFF_HEAD_DIM = 64
DIFF_SCALE = DIFF_HEAD_DIM ** -0.5
DIFF_WIDTH = DIFF_HEADS * 2 * DIFF_HEAD_DIM
MEM_HEADS = 4
MEM_HEAD_DIM = D_MODEL // MEM_HEADS
N_EXPERTS = 32
TOP_K = 4
D_FF = D_MODEL
SWIGLU_ALPHA = 1.702
SWIGLU_LIMIT = 7.0
NORM_EPS = 1e-6
NEG_INF = -1e30

LANES = 128
QCAT = KV_LORA_RANK + LANES
DAUG = 2 * LANES
POS_RADIX = 64
VMEM_LIMIT = 56 * 1024 * 1024

BF16 = jnp.bfloat16
F32 = jnp.float32


def _cparams(sem):
    return pltpu.CompilerParams(dimension_semantics=sem, vmem_limit_bytes=VMEM_LIMIT)


def _rms(x, g):
    return x * lax.rsqrt(jnp.mean(x * x, axis=-1, keepdims=True) + NORM_EPS) * g


def _dot(a, b):
    return jnp.dot(a, b, preferred_element_type=F32)


def _dot_nt(a, b):
    return lax.dot_general(a, b, (((1,), (1,)), ((), ())), preferred_element_type=F32)


ROW_SLABS = D_MODEL // LANES


def _rows_to_tiles(ref, x):
    n = x.shape[0]
    for s in range(ROW_SLABS):
        ref[pl.ds(s, n, stride=ROW_SLABS), :] = x[:, s * LANES:(s + 1) * LANES]


def _tiles_to_rows(ref, n):
    return jnp.concatenate([ref[pl.ds(s, n, stride=ROW_SLABS), :] for s in range(ROW_SLABS)], axis=1)


def _alibi_slope(h):
    return 2.0 ** (-8.0 * (h + 1.0) / DIFF_HEADS)


def _lambda(lam_ref, lam_init):
    v = lam_ref[...]
    a = jnp.sum(v[0:1] * v[1:2], axis=-1, keepdims=True)
    b = jnp.sum(v[2:3] * v[3:4], axis=-1, keepdims=True)
    return jnp.exp(a) - jnp.exp(b) + lam_init


def _proj_kernel(x_ref, cos_ref, sin_ref, aq_ref, ak_ref, ga_ref, wa_ref, gq_ref, wq_ref, gkv_ref, wuk_ref,
                 ckv_ref, kr_ref, dk_ref, dv_ref, qcat_ref, kvcat_ref, dqa_ref, dk1_ref, dk2_ref, dva_ref):
    dt = qcat_ref.dtype
    h = _rms(x_ref[...], ga_ref[...])
    u = _dot(h.astype(BF16), wa_ref[...])
    o = 0
    cq = u[:, o:o + Q_LORA_RANK]; o += Q_LORA_RANK
    ckv = u[:, o:o + KV_LORA_RANK]; o += KV_LORA_RANK
    krp = u[:, o:o + LANES]; o += LANES
    krs = u[:, o:o + LANES]; o += LANES
    dq = u[:, o:o + DIFF_WIDTH]; o += DIFF_WIDTH
    dk = u[:, o:o + DIFF_WIDTH]; o += DIFF_WIDTH
    dv = u[:, o:o + DIFF_WIDTH]
    cos = cos_ref[...]
    sin = sin_ref[...]
    k_rope = krp * cos + krs * sin
    c_kv = _rms(ckv, gkv_ref[...])
    ckv_ref[...] = c_kv
    kr_ref[...] = k_rope[:, :MLA_ROPE_DIM]
    dk_ref[...] = dk
    dv_ref[...] = dv
    kvcat_ref[:, :KV_LORA_RANK] = c_kv.astype(dt)
    kvcat_ref[:, KV_LORA_RANK:] = k_rope.astype(dt)
    q = _dot(_rms(cq, gq_ref[...]).astype(BF16), wq_ref[...])
    nq = MLA_HEADS * LANES
    lane = lax.broadcasted_iota(jnp.int32, (x_ref.shape[0], LANES), 1)
    ak = ak_ref[...]
    for hh in range(MLA_HEADS):
        sl = slice(hh * LANES, (hh + 1) * LANES)
        q_lat = _dot(q[:, sl].astype(BF16), wuk_ref[hh]) * MLA_SCALE
        q_rope = (q[:, nq + hh * LANES:nq + (hh + 1) * LANES] * cos
                  + q[:, 2 * nq + hh * LANES:2 * nq + (hh + 1) * LANES] * sin) * MLA_SCALE
        qcat_ref[hh, :, :KV_LORA_RANK] = q_lat.astype(dt)
        qcat_ref[hh, :, KV_LORA_RANK:] = q_rope.astype(dt)
    for hh in range(DIFF_HEADS):
        sl = slice(hh * LANES, (hh + 1) * LANES)
        base = hh * DAUG
        dqa_ref[:, base:base + LANES] = (dq[:, sl] * DIFF_SCALE).astype(dt)
        dqa_ref[:, base + LANES:base + DAUG] = aq_ref[:, sl].astype(dt)
        kh = dk[:, sl]
        dk1_ref[:, base:base + LANES] = jnp.where(lane < DIFF_HEAD_DIM, kh, 0.0).astype(dt)
        dk1_ref[:, base + LANES:base + DAUG] = ak.astype(dt)
        dk2_ref[:, base:base + LANES] = jnp.where(lane >= DIFF_HEAD_DIM, kh, 0.0).astype(dt)
        dk2_ref[:, base + LANES:base + DAUG] = ak.astype(dt)
        dva_ref[:, base:base + LANES] = dv[:, sl].astype(dt)
        dva_ref[:, base + LANES:base + DAUG] = jnp.ones((x_ref.shape[0], LANES), dt)


def _proj_weights(w_in, w_uq, w_ukv):
    d = w_in.shape[0]
    half = MLA_ROPE_DIM // 2
    edges = np.cumsum([0, Q_LORA_RANK, KV_LORA_RANK, MLA_ROPE_DIM, DIFF_WIDTH, DIFF_WIDTH, DIFF_WIDTH])
    cq, ckv, kr, dq, dk, dv = [w_in[:, edges[i]:edges[i + 1]] for i in range(6)]
    pad = jnp.zeros((d, LANES - MLA_ROPE_DIM), w_in.dtype)
    kr_sw = jnp.concatenate([kr[:, half:], kr[:, :half]], axis=1)
    wa = jnp.concatenate([cq, ckv, kr, pad, kr_sw, pad, dq, dk, dv], axis=1).astype(BF16)
    wq3 = w_uq.reshape(Q_LORA_RANK, MLA_HEADS, MLA_NOPE_DIM + MLA_ROPE_DIM)
    nope = wq3[:, :, :MLA_NOPE_DIM].reshape(Q_LORA_RANK, MLA_HEADS * MLA_NOPE_DIM)
    rp = wq3[:, :, MLA_NOPE_DIM:]
    pad3 = jnp.zeros((Q_LORA_RANK, MLA_HEADS, LANES - MLA_ROPE_DIM), w_uq.dtype)
    rope = jnp.concatenate([rp, pad3], axis=2).reshape(Q_LORA_RANK, MLA_HEADS * LANES)
    rope_sw = jnp.concatenate([rp[:, :, half:], rp[:, :, :half], pad3], axis=2).reshape(Q_LORA_RANK, MLA_HEADS * LANES)
    wq = jnp.concatenate([nope, rope, rope_sw], axis=1).astype(BF16)
    wkv3 = w_ukv.reshape(KV_LORA_RANK, MLA_HEADS, MLA_NOPE_DIM + MLA_V_DIM)
    wuk = jnp.transpose(wkv3[:, :, :MLA_NOPE_DIM], (1, 2, 0)).astype(BF16)
    wuv = jnp.transpose(wkv3[:, :, MLA_NOPE_DIM:], (1, 0, 2)).astype(BF16)
    return wa, wq, wuk, wuv


def _position_tables(pos):
    half = MLA_ROPE_DIM // 2
    inv = ROPE_BASE ** (-jnp.arange(half, dtype=F32) / half)
    ang = pos.astype(F32)[:, None] * inv
    cos, sin = jnp.cos(ang), jnp.sin(ang)
    z = jnp.zeros((pos.shape[0], LANES - MLA_ROPE_DIM), F32)
    cos2 = jnp.concatenate([cos, cos, z], axis=1)
    sin2 = jnp.concatenate([-sin, sin, z], axis=1)
    hi = (pos // POS_RADIX).astype(F32)[:, None]
    lo = (pos % POS_RADIX).astype(F32)[:, None]
    one = jnp.ones_like(hi)
    zc = jnp.zeros((pos.shape[0], LANES - 4), F32)
    ak = jnp.concatenate([hi, lo, one, one, zc], axis=1)
    aqs = []
    for h in range(DIFF_HEADS):
        s = _alibi_slope(h)
        aqs.append(jnp.concatenate([POS_RADIX * s * one, s * one, -POS_RADIX * s * hi, -s * lo, zc], axis=1))
    return cos2, sin2, jnp.concatenate(aqs, axis=1), ak


def _proj(x, pos, n_rep, attn_norm, wa, q_norm, wq, kv_norm, wuk, dt, tm):
    t, d = x.shape
    s = pos.shape[0]
    assert t == n_rep * s and s % tm == 0
    nsb = s // tm
    cos2, sin2, aq, ak = _position_tables(pos)
    row = lambda i: (i, 0)
    prow = lambda i: (i % nsb, 0)
    full = lambda i: (0, 0)
    ncol = wa.shape[1]
    outs = [
        jax.ShapeDtypeStruct((t, KV_LORA_RANK), F32),
        jax.ShapeDtypeStruct((t, MLA_ROPE_DIM), F32),
        jax.ShapeDtypeStruct((t, DIFF_WIDTH), F32),
        jax.ShapeDtypeStruct((t, DIFF_WIDTH), F32),
        jax.ShapeDtypeStruct((MLA_HEADS, t, QCAT), dt),
        jax.ShapeDtypeStruct((t, QCAT), dt),
        jax.ShapeDtypeStruct((t, DIFF_HEADS * DAUG), dt),
        jax.ShapeDtypeStruct((t, DIFF_HEADS * DAUG), dt),
        jax.ShapeDtypeStruct((t, DIFF_HEADS * DAUG), dt),
        jax.ShapeDtypeStruct((t, DIFF_HEADS * DAUG), dt),
    ]
    return pl.pallas_call(
        _proj_kernel,
        grid=(t // tm,),
        in_specs=[
            pl.BlockSpec((tm, d), row),
            pl.BlockSpec((tm, LANES), prow),
            pl.BlockSpec((tm, LANES), prow),
            pl.BlockSpec((tm, DIFF_HEADS * LANES), prow),
            pl.BlockSpec((tm, LANES), prow),
            pl.BlockSpec((1, d), full),
            pl.BlockSpec((d, ncol), full),
            pl.BlockSpec((1, Q_LORA_RANK), full),
            pl.BlockSpec(wq.shape, full),
            pl.BlockSpec((1, KV_LORA_RANK), full),
            pl.BlockSpec(wuk.shape, lambda i: (0, 0, 0)),
        ],
        out_specs=[
            pl.BlockSpec((tm, KV_LORA_RANK), row),
            pl.BlockSpec((tm, MLA_ROPE_DIM), row),
            pl.BlockSpec((tm, DIFF_WIDTH), row),
            pl.BlockSpec((tm, DIFF_WIDTH), row),
            pl.BlockSpec((MLA_HEADS, tm, QCAT), lambda i: (0, i, 0)),
            pl.BlockSpec((tm, QCAT), row),
            pl.BlockSpec((tm, DIFF_HEADS * DAUG), row),
            pl.BlockSpec((tm, DIFF_HEADS * DAUG), row),
            pl.BlockSpec((tm, DIFF_HEADS * DAUG), row),
            pl.BlockSpec((tm, DIFF_HEADS * DAUG), row),
        ],
        out_shape=outs,
        compiler_params=_cparams(("parallel",)),
        name="proj",
    )(x, cos2, sin2, aq, ak, attn_norm.reshape(1, d), wa, q_norm.reshape(1, -1), wq, kv_norm.reshape(1, -1), wuk)


def _online_update(s, v, m_ref, l_ref, acc_ref):
    m_prev = m_ref[...]
    m_new = jnp.maximum(m_prev, jnp.max(s, axis=-1, keepdims=True))
    alpha = jnp.exp(m_prev - m_new)
    p = jnp.exp(s - m_new)
    l_ref[...] = alpha * l_ref[...] + jnp.sum(p, axis=-1, keepdims=True)
    acc_ref[...] = alpha * acc_ref[...] + _dot(p.astype(BF16), v)
    m_ref[...] = m_new


def _lane_tile(x, width):
    n = width // LANES
    return x if n == 1 else jnp.concatenate([x] * n, axis=1)


def _flash_step(s, v, m_ref, l_ref, acc_ref):
    m_prev = m_ref[...]
    m_new = jnp.maximum(m_prev, jnp.max(s, axis=-1, keepdims=True))
    alpha = jnp.exp(m_prev - m_new)
    p = jnp.exp(s - _lane_tile(m_new, s.shape[1]))
    m_ref[...] = m_new
    if l_ref is not None:
        l_ref[...] = alpha * l_ref[...] + jnp.sum(p, axis=-1, keepdims=True)
    acc_ref[...] = _lane_tile(alpha, acc_ref.shape[1]) * acc_ref[...] + _dot(p.astype(BF16), v)


def _prompt_attn_kernel(qcat_ref, kvcat_ref, dqa_ref, dk1_ref, dk2_ref, dva_ref, lam_ref,
                        mla_ref, dif_ref, m_mla, l_mla, acc_mla, *dif_state, lam_init):
    m_dif = dif_state[0::2]
    acc_dif = dif_state[1::2]
    qi = pl.program_id(1)
    ki = pl.program_id(2)
    tq = dqa_ref.shape[0]
    tk = kvcat_ref.shape[0]

    @pl.when(ki == 0)
    def _():
        m_mla[...] = jnp.full(m_mla.shape, NEG_INF, F32)
        l_mla[...] = jnp.zeros(l_mla.shape, F32)
        acc_mla[...] = jnp.zeros(acc_mla.shape, F32)
        for m_ref, a_ref in zip(m_dif, acc_dif):
            m_ref[...] = jnp.full(m_ref.shape, NEG_INF, F32)
            a_ref[...] = jnp.zeros(a_ref.shape, F32)

    def step(masked):
        if masked:
            r = lax.broadcasted_iota(jnp.int32, (tq, tk), 0)
            c = lax.broadcasted_iota(jnp.int32, (tq, tk), 1)
            keep = c <= r
        kv = kvcat_ref[...]
        q = qcat_ref[...].reshape(MLA_HEADS * tq, QCAT)
        s = _dot_nt(q, kv)
        if masked:
            s = jnp.where(keep[None], s.reshape(MLA_HEADS, tq, tk), NEG_INF).reshape(MLA_HEADS * tq, tk)
        _flash_step(s, kv[:, :KV_LORA_RANK], m_mla, l_mla, acc_mla)
        for hh in range(DIFF_HEADS):
            cols = slice(hh * DAUG, (hh + 1) * DAUG)
            qa = dqa_ref[:, cols]
            v = dva_ref[:, cols]
            for mm, kref in enumerate((dk1_ref, dk2_ref)):
                s = _dot_nt(qa, kref[:, cols])
                if masked:
                    s = jnp.where(keep, s, NEG_INF)
                i = 2 * hh + mm
                _flash_step(s, v, m_dif[i], None, acc_dif[i])

    @pl.when(ki < qi)
    def _():
        step(False)

    @pl.when(ki == qi)
    def _():
        step(True)
        lam = _lambda(lam_ref, lam_init)
        for hh in range(MLA_HEADS):
            rows = slice(hh * tq, (hh + 1) * tq)
            o = acc_mla[rows, :] * _lane_tile(1.0 / l_mla[rows, :], KV_LORA_RANK)
            mla_ref[:, hh * KV_LORA_RANK:(hh + 1) * KV_LORA_RANK] = o.astype(mla_ref.dtype)
        for hh in range(DIFF_HEADS):
            a1 = acc_dif[2 * hh][...]
            a2 = acc_dif[2 * hh + 1][...]
            o1 = a1[:, :LANES] * (1.0 / a1[:, LANES:])
            o2 = a2[:, :LANES] * (1.0 / a2[:, LANES:])
            dif_ref[:, hh * LANES:(hh + 1) * LANES] = o1 - lam * o2


def _prompt_attn(qcat, kvcat, dqa, dk1, dk2, dva, lam_rows, lam_init, batch, seq, tq):
    nq = seq // tq
    qmap = lambda b, qi, ki: (b * nq + qi, 0)
    kmap = lambda b, qi, ki: (b * nq + jnp.minimum(ki, qi), 0)
    t = batch * seq
    dif_state = []
    for _ in range(2 * DIFF_HEADS):
        dif_state += [pltpu.VMEM((tq, LANES), F32), pltpu.VMEM((tq, DAUG), F32)]
    return pl.pallas_call(
        functools.partial(_prompt_attn_kernel, lam_init=lam_init),
        grid=(batch, nq, nq),
        in_specs=[
            pl.BlockSpec((MLA_HEADS, tq, QCAT), lambda b, qi, ki: (0, b * nq + qi, 0)),
            pl.BlockSpec((tq, QCAT), kmap),
            pl.BlockSpec((tq, DIFF_HEADS * DAUG), qmap),
            pl.BlockSpec((tq, DIFF_HEADS * DAUG), kmap),
            pl.BlockSpec((tq, DIFF_HEADS * DAUG), kmap),
            pl.BlockSpec((tq, DIFF_HEADS * DAUG), kmap),
            pl.BlockSpec((4, DIFF_HEAD_DIM), lambda b, qi, ki: (0, 0)),
        ],
        out_specs=[
            pl.BlockSpec((tq, MLA_HEADS * KV_LORA_RANK), qmap),
            pl.BlockSpec((tq, DIFF_WIDTH), qmap),
        ],
        out_shape=[
            jax.ShapeDtypeStruct((t, MLA_HEADS * KV_LORA_RANK), BF16),
            jax.ShapeDtypeStruct((t, DIFF_WIDTH), F32),
        ],
        scratch_shapes=[
            pltpu.VMEM((MLA_HEADS * tq, LANES), F32),
            pltpu.VMEM((MLA_HEADS * tq, LANES), F32),
            pltpu.VMEM((MLA_HEADS * tq, KV_LORA_RANK), F32),
        ] + dif_state,
        compiler_params=_cparams(("parallel", "parallel", "arbitrary")),
        name="prompt_attn",
    )(qcat, kvcat, dqa, dk1, dk2, dva, lam_rows)


def _sample_attn_kernel(pt_ref, qcat_ref, dqa_ref, nlat_ref, nkr_ref, ndk_ref, ndv_ref, lam_ref,
                        lat_hbm, kr_hbm, dk_hbm, dv_hbm,
                        mla_ref, dif_ref,
                        lat_buf, kr_buf, dk_buf, dv_buf, sems, qbd,
                        m_mla, l_mla, acc_mla, m_dif, l_dif, acc_dif,
                        *, lam_init, n_pages, chunk_pages, page, past):
    b = pl.program_id(0)
    c = pl.program_id(1)
    nb = pl.num_programs(0)
    nc = pl.num_programs(1)
    g = b * nc + c
    slot = g % 2
    dt = dqa_ref.shape[0]
    ck = chunk_pages * page
    rows_mla = MLA_HEADS * dt
    rows_dif = 2 * DIFF_HEADS * dt

    def copies(gi, sl):
        base = gi * chunk_pages
        out = []
        for j in range(chunk_pages):
            pid = pt_ref[base + j]
            hrows = pl.ds(j * page * DIFF_HEADS, page * DIFF_HEADS)
            out.append(pltpu.make_async_copy(lat_hbm.at[pid], lat_buf.at[sl, pl.ds(j * page, page)], sems.at[sl, 0]))
            out.append(pltpu.make_async_copy(kr_hbm.at[pid], kr_buf.at[sl, j], sems.at[sl, 1]))
            out.append(pltpu.make_async_copy(dk_hbm.at[pid], dk_buf.at[sl, hrows], sems.at[sl, 2]))
            out.append(pltpu.make_async_copy(dv_hbm.at[pid], dv_buf.at[sl, hrows], sems.at[sl, 3]))
        return out

    def heads_to_lanes(buf):
        return jnp.concatenate(
            [buf[pl.ds(hh, ck, stride=DIFF_HEADS), :] for hh in range(DIFF_HEADS)], axis=1).astype(BF16)

    @pl.when(g == 0)
    def _():
        for cp in copies(g, slot):
            cp.start()

    @pl.when(g + 1 < nb * nc)
    def _():
        for cp in copies(g + 1, 1 - slot):
            cp.start()

    row = lax.broadcasted_iota(jnp.int32, (rows_dif, 1), 0)
    tok_d = row % dt
    head_d = (row // dt) % DIFF_HEADS
    slope = jnp.exp2(-8.0 * (head_d.astype(F32) + 1.0) / DIFF_HEADS)

    @pl.when(c == 0)
    def _():
        m_mla[...] = jnp.full(m_mla.shape, NEG_INF, F32)
        l_mla[...] = jnp.zeros(l_mla.shape, F32)
        acc_mla[...] = jnp.zeros(acc_mla.shape, F32)
        m_dif[...] = jnp.full(m_dif.shape, NEG_INF, F32)
        l_dif[...] = jnp.zeros(l_dif.shape, F32)
        acc_dif[...] = jnp.zeros(acc_dif.shape, F32)
        dq = jnp.concatenate([dqa_ref[:, hh * DAUG:hh * DAUG + LANES] for hh in range(DIFF_HEADS)], axis=1)
        rep = jnp.broadcast_to(dq[None], (2 * DIFF_HEADS, dt, DIFF_WIDTH)).reshape(rows_dif, DIFF_WIDTH)
        col = lax.broadcasted_iota(jnp.int32, (rows_dif, DIFF_WIDTH), 1)
        own = (col // LANES == head_d) & ((col % LANES) // DIFF_HEAD_DIM == row // (DIFF_HEADS * dt))
        qbd[...] = jnp.where(own, rep, 0.0).astype(BF16)

    for cp in copies(g, slot):
        cp.wait()

    q = qcat_ref[...].reshape(rows_mla, QCAT)
    q_lat = q[:, :KV_LORA_RANK].astype(BF16)
    q_rope = q[:, KV_LORA_RANK:KV_LORA_RANK + MLA_ROPE_DIM].astype(BF16)

    lat = lat_buf[slot].astype(BF16)
    kr_t = jnp.concatenate([kr_buf[slot, j] for j in range(chunk_pages)], axis=1).astype(BF16)
    s = _dot_nt(q_lat, lat) + _dot(q_rope, kr_t)
    _online_update(s, lat, m_mla, l_mla, acc_mla)

    kd = heads_to_lanes(dk_buf.at[slot])
    vd = heads_to_lanes(dv_buf.at[slot])
    kcol = lax.broadcasted_iota(jnp.int32, (rows_dif, ck), 1)
    dist = (past + tok_d - c * ck) - kcol
    s = _dot_nt(qbd[...], kd) - slope * dist.astype(F32)
    _online_update(s, vd, m_dif, l_dif, acc_dif)

    @pl.when(c == nc - 1)
    def _():
        nlat = nlat_ref[...].astype(BF16)
        kcol_m = lax.broadcasted_iota(jnp.int32, (rows_mla, dt), 1)
        tok_m = lax.broadcasted_iota(jnp.int32, (rows_mla, dt), 0) % dt
        s = _dot_nt(q_lat, nlat) + _dot_nt(q_rope, nkr_ref[...].astype(BF16))
        s = jnp.where(kcol_m <= tok_m, s, NEG_INF)
        _online_update(s, nlat, m_mla, l_mla, acc_mla)
        kcol_n = lax.broadcasted_iota(jnp.int32, (rows_dif, dt), 1)
        s = _dot_nt(qbd[...], ndk_ref[...].astype(BF16)) - slope * (tok_d - kcol_n).astype(F32)
        s = jnp.where(kcol_n <= tok_d, s, NEG_INF)
        _online_update(s, ndv_ref[...].astype(BF16), m_dif, l_dif, acc_dif)

        lam = _lambda(lam_ref, lam_init)
        for hh in range(MLA_HEADS):
            rows = slice(hh * dt, (hh + 1) * dt)
            mla_ref[:, hh * KV_LORA_RANK:(hh + 1) * KV_LORA_RANK] = acc_mla[rows, :] * (1.0 / l_mla[rows, :])
        half = DIFF_HEADS * dt
        for hh in range(DIFF_HEADS):
            r1 = slice(hh * dt, (hh + 1) * dt)
            r2 = slice(half + hh * dt, half + (hh + 1) * dt)
            cols = slice(hh * LANES, (hh + 1) * LANES)
            o1 = acc_dif[r1, cols] * (1.0 / l_dif[r1, :])
            o2 = acc_dif[r2, cols] * (1.0 / l_dif[r2, :])
            dif_ref[:, cols] = o1 - lam * o2


def _sample_attn(page_table, qcat, dqa, nlat, nkr, ndk, ndv, lam_rows, pool_lat, pool_kr, pool_dk, pool_dv,
                 lam_init, nb, dt, chunk_pages):
    n_pages = page_table.shape[1]
    page = pool_lat.shape[1]
    assert n_pages % chunk_pages == 0
    nc = n_pages // chunk_pages
    past = n_pages * page
    t = nb * dt
    tok = lambda b, c, pt: (b, 0)
    any_spec = pl.BlockSpec(memory_space=pl.ANY)
    kern = functools.partial(_sample_attn_kernel, lam_init=lam_init, n_pages=n_pages,
                             chunk_pages=chunk_pages, page=page, past=past)
    grid_spec = pltpu.PrefetchScalarGridSpec(
        num_scalar_prefetch=1,
        grid=(nb, nc),
        in_specs=[
            pl.BlockSpec((MLA_HEADS, dt, QCAT), lambda b, c, pt: (0, b, 0)),
            pl.BlockSpec((dt, DIFF_HEADS * DAUG), tok),
            pl.BlockSpec((dt, KV_LORA_RANK), tok),
            pl.BlockSpec((dt, MLA_ROPE_DIM), tok),
            pl.BlockSpec((dt, DIFF_WIDTH), tok),
            pl.BlockSpec((dt, DIFF_WIDTH), tok),
            pl.BlockSpec((4, DIFF_HEAD_DIM), lambda b, c, pt: (0, 0)),
            any_spec, any_spec, any_spec, any_spec,
        ],
        out_specs=[
            pl.BlockSpec((dt, MLA_HEADS * KV_LORA_RANK), tok),
            pl.BlockSpec((dt, DIFF_WIDTH), tok),
        ],
        scratch_shapes=[
            pltpu.VMEM((2, chunk_pages * page, KV_LORA_RANK), F32),
            pltpu.VMEM((2, chunk_pages, MLA_ROPE_DIM, page), F32),
            pltpu.VMEM((2, chunk_pages * page * DIFF_HEADS, LANES), F32),
            pltpu.VMEM((2, chunk_pages * page * DIFF_HEADS, LANES), F32),
            pltpu.SemaphoreType.DMA((2, 4)),
            pltpu.VMEM((2 * DIFF_HEADS * dt, DIFF_WIDTH), BF16),
            pltpu.VMEM((MLA_HEADS * dt, 1), F32),
            pltpu.VMEM((MLA_HEADS * dt, 1), F32),
            pltpu.VMEM((MLA_HEADS * dt, KV_LORA_RANK), F32),
            pltpu.VMEM((2 * DIFF_HEADS * dt, 1), F32),
            pltpu.VMEM((2 * DIFF_HEADS * dt, 1), F32),
            pltpu.VMEM((2 * DIFF_HEADS * dt, DIFF_WIDTH), F32),
        ],
    )
    return pl.pallas_call(
        kern,
        grid_spec=grid_spec,
        out_shape=[
            jax.ShapeDtypeStruct((t, MLA_HEADS * KV_LORA_RANK), F32),
            jax.ShapeDtypeStruct((t, DIFF_WIDTH), F32),
        ],
        compiler_params=_cparams(("arbitrary", "arbitrary")),
        name="sample_attn",
    )(page_table.reshape(-1), qcat, dqa, nlat, nkr, ndk, ndv, lam_rows, pool_lat, pool_kr, pool_dk, pool_dv)


def _mix_out_kernel(lat_ref, dif_ref, x_ref, wuv_ref, gs_ref, wo_ref, gm_ref, wmq_ref,
                    x1_ref, qm_ref, *, lam_init):
    parts = []
    for hh in range(MLA_HEADS):
        lat = lat_ref[:, hh * KV_LORA_RANK:(hh + 1) * KV_LORA_RANK].astype(BF16)
        parts.append(_dot(lat, wuv_ref[hh]))
    gs = gs_ref[...]
    for hh in range(DIFF_HEADS):
        parts.append(_rms(dif_ref[:, hh * LANES:(hh + 1) * LANES], gs) * (1.0 - lam_init))
    mix = jnp.concatenate(parts, axis=1).astype(BF16)
    x1 = x_ref[...] + _dot(mix, wo_ref[...])
    x1_ref[...] = x1
    hq = _rms(x1, gm_ref[...]).astype(BF16)
    qm_ref[...] = (_dot(hq, wmq_ref[...]) * MEM_HEAD_DIM ** -0.5).astype(qm_ref.dtype)


def _mix_out(mla_lat, diff_o, x, wuv, subln, wo, gm, wmq, lam_init, dt, tm):
    t, d = x.shape
    row = lambda i: (i, 0)
    full = lambda i: (0, 0)
    return pl.pallas_call(
        functools.partial(_mix_out_kernel, lam_init=lam_init),
        grid=(t // tm,),
        in_specs=[
            pl.BlockSpec((tm, MLA_HEADS * KV_LORA_RANK), row),
            pl.BlockSpec((tm, DIFF_WIDTH), row),
            pl.BlockSpec((tm, d), row),
            pl.BlockSpec(wuv.shape, lambda i: (0, 0, 0)),
            pl.BlockSpec((1, LANES), full),
            pl.BlockSpec(wo.shape, full),
            pl.BlockSpec((1, d), full),
            pl.BlockSpec(wmq.shape, full),
        ],
        out_specs=[pl.BlockSpec((tm, d), row), pl.BlockSpec((tm, d), row)],
        out_shape=[jax.ShapeDtypeStruct((t, d), F32), jax.ShapeDtypeStruct((t, d), dt)],
        compiler_params=_cparams(("parallel",)),
        name="mix_out",
    )(mla_lat, diff_o, x, wuv, subln.reshape(1, -1), wo, gm.reshape(1, d), wmq)


def _mem_kv_kernel(mem_ref, g_ref, wk_ref, wv_ref, k_ref, v_ref):
    mn = _rms(mem_ref[...], g_ref[...]).astype(BF16)
    k_ref[...] = _dot(mn, wk_ref[...])
    v_ref[...] = _dot(mn, wv_ref[...])


def _mem_kv(mem, g, wk, wv, tm):
    t, d = mem.shape
    row = lambda i: (i, 0)
    full = lambda i: (0, 0)
    return pl.pallas_call(
        _mem_kv_kernel,
        grid=(t // tm,),
        in_specs=[pl.BlockSpec((tm, d), row), pl.BlockSpec((1, d), full),
                  pl.BlockSpec(wk.shape, full), pl.BlockSpec(wv.shape, full)],
        out_specs=[pl.BlockSpec((tm, d), row), pl.BlockSpec((tm, d), row)],
        out_shape=[jax.ShapeDtypeStruct((t, d), F32), jax.ShapeDtypeStruct((t, d), F32)],
        compiler_params=_cparams(("parallel",)),
        name="mem_kv",
    )(mem, g.reshape(1, d), wk, wv)


def _mem_attn_kernel(q_ref, k_ref, v_ref, o_ref):
    for hh in range(MEM_HEADS):
        cols = slice(hh * MEM_HEAD_DIM, (hh + 1) * MEM_HEAD_DIM)
        q = q_ref[:, cols].astype(BF16)
        k = k_ref[0, :, cols].astype(BF16)
        v = v_ref[0, :, cols].astype(BF16)
        s = _dot_nt(q, k)
        p = jnp.exp(s - jnp.max(s, axis=-1, keepdims=True))
        o = _dot(p.astype(BF16), v) * (1.0 / jnp.sum(p, axis=-1, keepdims=True))
        o_ref[:, cols] = o.astype(o_ref.dtype)


def _mem_attn(qm, mk, mv, nb, rows_per_batch, tq, dt):
    t, d = qm.shape
    nqb = rows_per_batch // tq
    m = mk.shape[1]
    return pl.pallas_call(
        _mem_attn_kernel,
        grid=(nb, nqb),
        in_specs=[
            pl.BlockSpec((tq, d), lambda b, i: (b * nqb + i, 0)),
            pl.BlockSpec((1, m, d), lambda b, i: (b, 0, 0)),
            pl.BlockSpec((1, m, d), lambda b, i: (b, 0, 0)),
        ],
        out_specs=pl.BlockSpec((tq, d), lambda b, i: (b * nqb + i, 0)),
        out_shape=jax.ShapeDtypeStruct((t, d), dt),
        compiler_params=_cparams(("parallel", "parallel")),
        name="mem_attn",
    )(qm, mk, mv)


def _post_attn_kernel(om_ref, x1_ref, wmo_ref, gf_ref, wr_ref, br_ref, cnt_in_ref,
                      x2_ref, hn_ref, idx_ref, gate_ref, rank_ref, cnt_ref, carry):
    tm = om_ref.shape[0]

    @pl.when(pl.program_id(0) == 0)
    def _():
        carry[...] = cnt_in_ref[...]

    x2 = x1_ref[...] + _dot(om_ref[...].astype(BF16), wmo_ref[...])
    x2_ref[...] = x2
    hn = _rms(x2, gf_ref[...])
    _rows_to_tiles(hn_ref, hn)
    logits = jnp.dot(hn, wr_ref[...], preferred_element_type=F32, precision=lax.Precision.HIGHEST) + br_ref[...]
    lane_e = lax.broadcasted_iota(jnp.int32, (tm, N_EXPERTS), 1)
    lane_o = lax.broadcasted_iota(jnp.int32, (tm, LANES), 1)
    work = logits
    vals, hots = [], []
    idx_out = jnp.zeros((tm, LANES), jnp.int32)
    for k in range(TOP_K):
        mx = jnp.max(work, axis=-1, keepdims=True)
        idx = jnp.min(jnp.where(work == mx, lane_e, N_EXPERTS), axis=-1, keepdims=True)
        hot = lane_e == idx
        vals.append(mx)
        hots.append(hot)
        idx_out = jnp.where(lane_o == k, idx, idx_out)
        work = jnp.where(hot, -jnp.inf, work)
    es = [jnp.exp(v - vals[0]) for v in vals]
    den = es[0] + es[1] + es[2] + es[3]
    gate_out = jnp.zeros((tm, LANES), F32)
    for k in range(TOP_K):
        gate_out = jnp.where(lane_o == k, es[k] * (1.0 / den), gate_out)
    multi = (hots[0] | hots[1] | hots[2] | hots[3]).astype(F32)
    r = lax.broadcasted_iota(jnp.int32, (tm, tm), 0)
    cc = lax.broadcasted_iota(jnp.int32, (tm, tm), 1)
    below = (cc < r).astype(BF16)
    base = carry[...] + _dot(below, multi.astype(BF16))
    rank_out = jnp.zeros((tm, LANES), jnp.int32)
    for k in range(TOP_K):
        rk = jnp.sum(jnp.where(hots[k], base, 0.0), axis=-1, keepdims=True).astype(jnp.int32)
        rank_out = jnp.where(lane_o == k, rk, rank_out)
    idx_ref[...] = idx_out
    gate_ref[...] = gate_out
    rank_ref[...] = rank_out
    new = carry[...] + jnp.sum(multi, axis=0, keepdims=True)
    carry[...] = new
    cnt_ref[...] = new


def _post_attn(om, x1, wmo, gf, wr, br, cnt_in, tm):
    t, d = x1.shape
    row = lambda i: (i, 0)
    full = lambda i: (0, 0)
    return pl.pallas_call(
        _post_attn_kernel,
        grid=(t // tm,),
        in_specs=[
            pl.BlockSpec((tm, d), row), pl.BlockSpec((tm, d), row),
            pl.BlockSpec(wmo.shape, full), pl.BlockSpec((1, d), full),
            pl.BlockSpec(wr.shape, full), pl.BlockSpec((1, N_EXPERTS), full),
            pl.BlockSpec((1, N_EXPERTS), full),
        ],
        out_specs=[
            pl.BlockSpec((tm, d), row), pl.BlockSpec((tm * ROW_SLABS, LANES), row),
            pl.BlockSpec((tm, LANES), row), pl.BlockSpec((tm, LANES), row), pl.BlockSpec((tm, LANES), row),
            pl.BlockSpec((1, N_EXPERTS), full),
        ],
        out_shape=[
            jax.ShapeDtypeStruct((t, d), F32), jax.ShapeDtypeStruct((t * ROW_SLABS, LANES), F32),
            jax.ShapeDtypeStruct((t, LANES), jnp.int32), jax.ShapeDtypeStruct((t, LANES), F32),
            jax.ShapeDtypeStruct((t, LANES), jnp.int32),
            jax.ShapeDtypeStruct((1, N_EXPERTS), F32),
        ],
        scratch_shapes=[pltpu.VMEM((1, N_EXPERTS), F32)],
        compiler_params=_cparams(("arbitrary",)),
        name="post_attn",
    )(om, x1, wmo, gf.reshape(1, d), wr, br.reshape(1, N_EXPERTS), cnt_in)


MXU_TILE = 256


def _deinterleave_matrix():
    p = np.zeros((MXU_TILE, MXU_TILE), np.float32)
    half = MXU_TILE // 2
    for i in range(half):
        p[2 * i, i] = 1.0
        p[2 * i + 1, half + i] = 1.0
    return jnp.asarray(p, BF16)


def _moe_prep_kernel(w1_ref, w2_ref, perm_ref, w1g_ref, w1l_ref, w2b_ref):
    half = MXU_TILE // 2
    perm = perm_ref[...]
    for cb in range(w1_ref.shape[2] // MXU_TILE):
        blk = w1_ref[0, :, cb * MXU_TILE:(cb + 1) * MXU_TILE].astype(BF16)
        r = _dot(blk, perm)
        w1g_ref[0, :, cb * half:(cb + 1) * half] = r[:, :half].astype(BF16)
        w1l_ref[0, :, cb * half:(cb + 1) * half] = r[:, half:].astype(BF16)
    w2b_ref[...] = w2_ref[...].astype(BF16)


def _moe_prep(w1, w2):
    e, d, f2 = w1.shape
    f = f2 // 2
    blk = lambda i: (i, 0, 0)
    return pl.pallas_call(
        _moe_prep_kernel,
        grid=(e,),
        in_specs=[pl.BlockSpec((1, d, f2), blk), pl.BlockSpec((1, f, d), blk),
                  pl.BlockSpec((MXU_TILE, MXU_TILE), lambda i: (0, 0))],
        out_specs=[pl.BlockSpec((1, d, f), blk), pl.BlockSpec((1, d, f), blk), pl.BlockSpec((1, f, d), blk)],
        out_shape=[jax.ShapeDtypeStruct((e, d, f), BF16), jax.ShapeDtypeStruct((e, d, f), BF16),
                   jax.ShapeDtypeStruct((e, f, d), BF16)],
        compiler_params=_cparams(("parallel",)),
        name="moe_prep",
    )(w1, w2, _deinterleave_matrix())


def _dispatch_kernel(dest_ref, hn_ref, xs_in_ref, xs_ref, sem):
    del xs_in_ref
    tm = hn_ref.shape[0] // ROW_SLABS
    base = pl.program_id(0) * tm * TOP_K

    def copy(j):
        src = pl.multiple_of((j // TOP_K) * ROW_SLABS, ROW_SLABS)
        dst = pl.multiple_of(dest_ref[base + j] * ROW_SLABS, ROW_SLABS)
        return pltpu.make_async_copy(hn_ref.at[pl.ds(src, ROW_SLABS)], xs_ref.at[pl.ds(dst, ROW_SLABS)], sem)

    def start(j, carry):
        copy(j).start()
        return carry

    def wait(j, carry):
        copy(j).wait()
        return carry

    lax.fori_loop(0, tm * TOP_K, start, 0)
    lax.fori_loop(0, tm * TOP_K, wait, 0)


def _dispatch(dest_flat, hn, xs, tm):
    t = hn.shape[0] // ROW_SLABS
    grid_spec = pltpu.PrefetchScalarGridSpec(
        num_scalar_prefetch=1,
        grid=(t // tm,),
        in_specs=[pl.BlockSpec((tm * ROW_SLABS, LANES), lambda i, dest: (i, 0)), pl.BlockSpec(memory_space=pl.ANY)],
        out_specs=pl.BlockSpec(memory_space=pl.ANY),
        scratch_shapes=[pltpu.SemaphoreType.DMA(())],
    )
    return pl.pallas_call(
        _dispatch_kernel,
        grid_spec=grid_spec,
        out_shape=jax.ShapeDtypeStruct(xs.shape, xs.dtype),
        input_output_aliases={2: 0},
        compiler_params=pltpu.CompilerParams(dimension_semantics=("arbitrary",), vmem_limit_bytes=VMEM_LIMIT,
                                             has_side_effects=True),
        name="dispatch",
    )(dest_flat, hn, xs)


def _experts_kernel(te_ref, tv_ref, x_ref, w1g_ref, w1l_ref, w2_ref, b1g_ref, b1l_ref, b2_ref, y_ref):
    @pl.when(tv_ref[pl.program_id(0)] > 0)
    def _():
        x = _tiles_to_rows(x_ref, x_ref.shape[0] // ROW_SLABS).astype(BF16)
        glu = jnp.minimum(_dot(x, w1g_ref[0]) + b1g_ref[0], SWIGLU_LIMIT)
        lin = jnp.clip(_dot(x, w1l_ref[0]) + b1l_ref[0], -SWIGLU_LIMIT, SWIGLU_LIMIT)
        a = glu * jax.nn.sigmoid(SWIGLU_ALPHA * glu) * (lin + 1.0)
        _rows_to_tiles(y_ref, _dot(a.astype(BF16), w2_ref[0]) + b2_ref[0])

    @pl.when(tv_ref[pl.program_id(0)] == 0)
    def _():
        y_ref[...] = jnp.zeros(y_ref.shape, F32)


def _experts(tile_expert, tile_valid, xs, w1g, w1l, w2b, b1g, b1l, b2, tmoe):
    r = xs.shape[0] // ROW_SLABS
    d, f = w1g.shape[1:]
    wmap = lambda g, te, tv: (te[g], 0, 0)
    grid_spec = pltpu.PrefetchScalarGridSpec(
        num_scalar_prefetch=2,
        grid=(r // tmoe,),
        in_specs=[
            pl.BlockSpec((tmoe * ROW_SLABS, LANES), lambda g, te, tv: (g, 0)),
            pl.BlockSpec((1, d, f), wmap), pl.BlockSpec((1, d, f), wmap), pl.BlockSpec((1, f, d), wmap),
            pl.BlockSpec((1, 1, f), wmap), pl.BlockSpec((1, 1, f), wmap), pl.BlockSpec((1, 1, d), wmap),
        ],
        out_specs=pl.BlockSpec((tmoe * ROW_SLABS, LANES), lambda g, te, tv: (g, 0)),
    )
    return pl.pallas_call(
        _experts_kernel,
        grid_spec=grid_spec,
        out_shape=jax.ShapeDtypeStruct(xs.shape, F32),
        compiler_params=_cparams(("arbitrary",)),
        name="moe_experts",
    )(tile_expert, tile_valid, xs, w1g, w1l, w2b, b1g, b1l, b2)


def _combine_kernel(dest_ref, x2_ref, gate_ref, gf_ref, ys_hbm, y_ref, buf, sems):
    tm = x2_ref.shape[0]
    i = pl.program_id(0)
    slot = i % 2

    def copy(tile, sl, j):
        src = pl.multiple_of(dest_ref[tile * (tm * TOP_K) + j] * ROW_SLABS, ROW_SLABS)
        dst = pl.multiple_of((j // TOP_K) * ROW_SLABS, ROW_SLABS)
        return pltpu.make_async_copy(ys_hbm.at[pl.ds(src, ROW_SLABS)],
                                     buf.at[sl, j % TOP_K, pl.ds(dst, ROW_SLABS)], sems.at[sl])

    def start_tile(tile, sl):
        def body(j, carry):
            copy(tile, sl, j).start()
            return carry
        lax.fori_loop(0, tm * TOP_K, body, 0)

    @pl.when(i == 0)
    def _():
        start_tile(i, slot)

    @pl.when(i + 1 < pl.num_programs(0))
    def _():
        start_tile(i + 1, 1 - slot)

    def wait(j, carry):
        copy(i, slot, j).wait()
        return carry

    lax.fori_loop(0, tm * TOP_K, wait, 0)
    gate = gate_ref[...]
    parts = []
    for s in range(ROW_SLABS):
        acc = x2_ref[:, s * LANES:(s + 1) * LANES]
        for k in range(TOP_K):
            acc = acc + gate[:, k:k + 1] * buf.at[slot, k][pl.ds(s, tm, stride=ROW_SLABS), :]
        parts.append(acc)
    y_ref[...] = _rms(jnp.concatenate(parts, axis=1), gf_ref[...])


def _combine(dest_flat, x2, gate, final_norm, ys, tm):
    t, d = x2.shape
    grid_spec = pltpu.PrefetchScalarGridSpec(
        num_scalar_prefetch=1,
        grid=(t // tm,),
        in_specs=[
            pl.BlockSpec((tm, d), lambda i, dest: (i, 0)),
            pl.BlockSpec((tm, LANES), lambda i, dest: (i, 0)),
            pl.BlockSpec((1, d), lambda i, dest: (0, 0)),
            pl.BlockSpec(memory_space=pl.ANY),
        ],
        out_specs=pl.BlockSpec((tm, d), lambda i, dest: (i, 0)),
        scratch_shapes=[pltpu.VMEM((2, TOP_K, tm * ROW_SLABS, LANES), F32), pltpu.SemaphoreType.DMA((2,))],
    )
    return pl.pallas_call(
        _combine_kernel,
        grid_spec=grid_spec,
        out_shape=jax.ShapeDtypeStruct((t, d), F32),
        compiler_params=_cparams(("arbitrary",)),
        name="combine",
    )(dest_flat, x2, gate, final_norm.reshape(1, d), ys)


def _slot_plan(counts, n_tiles, tmoe):
    counts = counts.reshape(-1).astype(jnp.int32)
    tiles = (counts + tmoe - 1) // tmoe
    ends = jnp.cumsum(tiles)
    row_start = (ends - tiles) * tmoe
    g = jnp.arange(n_tiles, dtype=jnp.int32)
    total = ends[-1]
    last_expert = jnp.max(jnp.where(counts > 0, jnp.arange(N_EXPERTS, dtype=jnp.int32), 0))
    te = jnp.sum((g[:, None] >= ends[None, :]).astype(jnp.int32), axis=1)
    tile_expert = jnp.where(g < total, jnp.minimum(te, N_EXPERTS - 1), last_expert).astype(jnp.int32)
    tile_valid = (g < total).astype(jnp.int32)
    return row_start, tile_expert, tile_valid


TM_TOKENS = 256
TQ_ATTN = 512
TM_ROUTE = 256
TM_EXPERT = 256
CHUNK_PAGES = 16


def kernel(x_prompt, x_sample, cache_mla_latent, cache_mla_krope, cache_diff_k, cache_diff_v, cache_mem_k, cache_mem_v, page_table, mem_prompt, attn_norm, w_in, q_norm, w_uq, kv_norm, w_ukv, lambda_q1, lambda_k1, lambda_q2, lambda_k2, subln_norm, w_o, mem_q_norm, mem_kv_norm, w_mq, w_mk, w_mv, w_mo, ffn_norm, w_router, b_router, w_mlp1, b_mlp1, w_mlp2, b_mlp2, final_norm):
    depth = w_in.shape[0]
    assert depth == 1
    for h in range(DIFF_HEADS):
        assert math.frexp(_alibi_slope(h))[0] == 0.5
    l = 0
    lam_init = 0.8 - 0.6 * math.exp(-0.3 * l)
    nbp, seq, d = x_prompt.shape
    nbs, dseq, _ = x_sample.shape
    n_phys, page = cache_mla_latent.shape[1:3]
    n_pages = page_table.shape[1]
    past = n_pages * page
    mem_len = mem_prompt.shape[1]
    assert seq <= POS_RADIX * POS_RADIX and page == LANES and d == D_MODEL
    tp, ts = nbp * seq, nbs * dseq

    wa, wq, wuk, wuv = _proj_weights(w_in[l], w_uq[l], w_ukv[l])
    lam_rows = jnp.stack([lambda_q1[l], lambda_k1[l], lambda_q2[l], lambda_k2[l]]).astype(F32)
    wo = w_o[l].astype(BF16)
    wmq = w_mq[l].astype(BF16)
    wmo = w_mo[l].astype(BF16)

    xp = x_prompt.reshape(tp, d)
    tm_p = min(TM_TOKENS, seq)
    (p_ckv, p_kr, p_dk, p_dv, p_qcat, p_kvcat, p_dqa, p_dk1, p_dk2, p_dva) = _proj(
        xp, jnp.arange(seq, dtype=jnp.int32), nbp, attn_norm[l], wa, q_norm[l], wq, kv_norm[l], wuk, BF16, tm_p)
    p_lat, p_dif = _prompt_attn(p_qcat, p_kvcat, p_dqa, p_dk1, p_dk2, p_dva, lam_rows, lam_init,
                                nbp, seq, min(TQ_ATTN, seq))
    p_x1, p_qm = _mix_out(p_lat, p_dif, xp, wuv, subln_norm[l], wo, mem_q_norm[l], wmq, lam_init, BF16, tm_p)
    mk, mv = _mem_kv(mem_prompt.reshape(nbp * mem_len, d), mem_kv_norm[l], w_mk[l].astype(BF16),
                     w_mv[l].astype(BF16), min(TM_TOKENS, nbp * mem_len))
    p_om = _mem_attn(p_qm, mk.reshape(nbp, mem_len, d), mv.reshape(nbp, mem_len, d), nbp, seq, tm_p, BF16)

    xs_tok = x_sample.reshape(ts, d)
    tm_s = min(TM_TOKENS, ts)
    pos_s = past + jnp.arange(dseq, dtype=jnp.int32)
    (s_ckv, s_kr, s_dk, s_dv, s_qcat, _, s_dqa, _, _, _) = _proj(
        xs_tok, jnp.tile(pos_s, tm_s // dseq), ts // tm_s, attn_norm[l], wa, q_norm[l], wq, kv_norm[l], wuk,
        F32, tm_s)
    s_lat, s_dif = _sample_attn(
        page_table, s_qcat, s_dqa, s_ckv, s_kr, s_dk, s_dv, lam_rows,
        cache_mla_latent.reshape(n_phys, page, KV_LORA_RANK),
        jnp.swapaxes(cache_mla_krope.reshape(n_phys, page, MLA_ROPE_DIM), 1, 2),
        cache_diff_k.reshape(n_phys, page * DIFF_HEADS, LANES), cache_diff_v.reshape(n_phys, page * DIFF_HEADS, LANES),
        lam_init, nbs, dseq, min(CHUNK_PAGES, n_pages))
    s_x1, s_qm = _mix_out(s_lat, s_dif, xs_tok, wuv, subln_norm[l], wo, mem_q_norm[l], wmq, lam_init, F32, tm_s)
    s_om = _mem_attn(s_qm, cache_mem_k.reshape(nbs, mem_len, d), cache_mem_v.reshape(nbs, mem_len, d),
                     nbs, dseq, dseq, F32)

    tr_p, tr_s = min(TM_ROUTE, tp), min(TM_ROUTE, ts)
    zero_cnt = jnp.zeros((1, N_EXPERTS), F32)
    p_x2, p_hn, p_idx, p_gate, p_rank, cnt_p = _post_attn(p_om, p_x1, wmo, ffn_norm[l], w_router[l], b_router[l],
                                                          zero_cnt, tr_p)
    s_x2, s_hn, s_idx, s_gate, s_rank, cnt = _post_attn(s_om, s_x1, wmo, ffn_norm[l], w_router[l], b_router[l],
                                                        cnt_p, tr_s)
    n_pairs = (tp + ts) * TOP_K
    n_tiles = (n_pairs + N_EXPERTS * (TM_EXPERT - 1)) // TM_EXPERT + 1
    row_start, tile_expert, tile_valid = _slot_plan(cnt, n_tiles, TM_EXPERT)
    p_dest = (row_start[p_idx[:, :TOP_K]] + p_rank[:, :TOP_K]).reshape(-1)
    s_dest = (row_start[s_idx[:, :TOP_K]] + s_rank[:, :TOP_K]).reshape(-1)

    w1g, w1l, w2b = _moe_prep(w_mlp1.reshape(w_mlp1.shape[1:]), w_mlp2.reshape(w_mlp2.shape[1:]))
    b1 = b_mlp1[l].reshape(N_EXPERTS, D_FF, 2)
    b1g = b1[:, :, 0].reshape(N_EXPERTS, 1, D_FF)
    b1l = b1[:, :, 1].reshape(N_EXPERTS, 1, D_FF)
    b2 = b_mlp2[l].reshape(N_EXPERTS, 1, d)
    slots = jnp.zeros((n_tiles * TM_EXPERT * ROW_SLABS, LANES), F32)
    slots = _dispatch(p_dest, p_hn, slots, tr_p)
    slots = _dispatch(s_dest, s_hn, slots, tr_s)
    ys = _experts(tile_expert, tile_valid, slots, w1g, w1l, w2b, b1g, b1l, b2, TM_EXPERT)
    y_p = _combine(p_dest, p_x2, p_gate, final_norm, ys, tr_p)
    y_s = _combine(s_dest, s_x2, s_gate, final_norm, ys, tr_s)

    dshape = (DIFF_HEADS, 2 * DIFF_HEAD_DIM)
    mshape = (MEM_HEADS, MEM_HEAD_DIM)
    return (
        y_p.reshape(nbp, seq, d),
        y_s.reshape(nbs, dseq, d),
        p_ckv.reshape(1, nbp, seq, KV_LORA_RANK),
        p_kr.reshape(1, nbp, seq, MLA_ROPE_DIM),
        p_dk.reshape(1, nbp, seq, *dshape),
        p_dv.reshape(1, nbp, seq, *dshape),
        mk.reshape(1, nbp, mem_len, *mshape),
        mv.reshape(1, nbp, mem_len, *mshape),
        s_ckv.reshape(1, nbs, dseq, KV_LORA_RANK),
        s_kr.reshape(1, nbs, dseq, MLA_ROPE_DIM),
        s_dk.reshape(1, nbs, dseq, *dshape),
        s_dv.reshape(1, nbs, dseq, *dshape),
    )
```

```python
import functools
import math

import numpy as np
import jax
import jax.numpy as jnp
from jax import lax
from jax.experimental import pallas as pl
from jax.experimental.pallas import tpu as pltpu

D_MODEL = 1024
MLA_HEADS = 4
MLA_NOPE_DIM = 128
MLA_ROPE_DIM = 64
MLA_V_DIM = 128
Q_LORA_RANK = 384
KV_LORA_RANK = 256
ROPE_BASE = 10000.0
MLA_SCALE = (MLA_NOPE_DIM + MLA_ROPE_DIM) ** -0.5
DIFF_HEADS = 4
DIFF_HEAD_DIM = 64
DIFF_SCALE = DIFF_HEAD_DIM ** -0.5
DIFF_WIDTH = DIFF_HEADS * 2 * DIFF_HEAD_DIM
MEM_HEADS = 4
MEM_HEAD_DIM = D_MODEL // MEM_HEADS
N_EXPERTS = 32
TOP_K = 4
D_FF = D_MODEL
SWIGLU_ALPHA = 1.702
SWIGLU_LIMIT = 7.0
NORM_EPS = 1e-6
NEG_INF = -1e30

LANES = 128
QCAT = KV_LORA_RANK + LANES
DAUG = 2 * LANES
POS_RADIX = 64
VMEM_LIMIT = 56 * 1024 * 1024

BF16 = jnp.bfloat16
F32 = jnp.float32


def _cparams(sem):
    return pltpu.CompilerParams(dimension_semantics=sem, vmem_limit_bytes=VMEM_LIMIT)


def _rms(x, g):
    return x * lax.rsqrt(jnp.mean(x * x, axis=-1, keepdims=True) + NORM_EPS) * g


def _dot(a, b):
    return jnp.dot(a, b, preferred_element_type=F32)


def _dot_nt(a, b):
    return lax.dot_general(a, b, (((1,), (1,)), ((), ())), preferred_element_type=F32)


ROW_SLABS = D_MODEL // LANES


def _rows_to_tiles(ref, x):
    n = x.shape[0]
    for s in range(ROW_SLABS):
        ref[pl.ds(s, n, stride=ROW_SLABS), :] = x[:, s * LANES:(s + 1) * LANES]


def _tiles_to_rows(ref, n):
    return jnp.concatenate([ref[pl.ds(s, n, stride=ROW_SLABS), :] for s in range(ROW_SLABS)], axis=1)


def _alibi_slope(h):
    return 2.0 ** (-8.0 * (h + 1.0) / DIFF_HEADS)


def _lambda(lam_ref, lam_init):
    v = lam_ref[...]
    a = jnp.sum(v[0:1] * v[1:2], axis=-1, keepdims=True)
    b = jnp.sum(v[2:3] * v[3:4], axis=-1, keepdims=True)
    return jnp.exp(a) - jnp.exp(b) + lam_init


def _proj_kernel(x_ref, cos_ref, sin_ref, aq_ref, ak_ref, ga_ref, wa_ref, gq_ref, wq_ref, gkv_ref, wuk_ref,
                 ckv_ref, kr_ref, dk_ref, dv_ref, qcat_ref, kvcat_ref, dqa_ref, dk1_ref, dk2_ref, dva_ref):
    dt = qcat_ref.dtype
    h = _rms(x_ref[...], ga_ref[...])
    u = _dot(h.astype(BF16), wa_ref[...])
    o = 0
    cq = u[:, o:o + Q_LORA_RANK]; o += Q_LORA_RANK
    ckv = u[:, o:o + KV_LORA_RANK]; o += KV_LORA_RANK
    krp = u[:, o:o + LANES]; o += LANES
    krs = u[:, o:o + LANES]; o += LANES
    dq = u[:, o:o + DIFF_WIDTH]; o += DIFF_WIDTH
    dk = u[:, o:o + DIFF_WIDTH]; o += DIFF_WIDTH
    dv = u[:, o:o + DIFF_WIDTH]
    cos = cos_ref[...]
    sin = sin_ref[...]
    k_rope = krp * cos + krs * sin
    c_kv = _rms(ckv, gkv_ref[...])
    ckv_ref[...] = c_kv
    kr_ref[...] = k_rope[:, :MLA_ROPE_DIM]
    dk_ref[...] = dk
    dv_ref[...] = dv
    kvcat_ref[:, :KV_LORA_RANK] = c_kv.astype(dt)
    kvcat_ref[:, KV_LORA_RANK:] = k_rope.astype(dt)
    q = _dot(_rms(cq, gq_ref[...]).astype(BF16), wq_ref[...])
    nq = MLA_HEADS * LANES
    lane = lax.broadcasted_iota(jnp.int32, (x_ref.shape[0], LANES), 1)
    ak = ak_ref[...]
    for hh in range(MLA_HEADS):
        sl = slice(hh * LANES, (hh + 1) * LANES)
        q_lat = _dot(q[:, sl].astype(BF16), wuk_ref[hh]) * MLA_SCALE
        q_rope = (q[:, nq + hh * LANES:nq + (hh + 1) * LANES] * cos
                  + q[:, 2 * nq + hh * LANES:2 * nq + (hh + 1) * LANES] * sin) * MLA_SCALE
        qcat_ref[hh, :, :KV_LORA_RANK] = q_lat.astype(dt)
        qcat_ref[hh, :, KV_LORA_RANK:] = q_rope.astype(dt)
    for hh in range(DIFF_HEADS):
        sl = slice(hh * LANES, (hh + 1) * LANES)
        base = hh * DAUG
        dqa_ref[:, base:base + LANES] = (dq[:, sl] * DIFF_SCALE).astype(dt)
        dqa_ref[:, base + LANES:base + DAUG] = aq_ref[:, sl].astype(dt)
        kh = dk[:, sl]
        dk1_ref[:, base:base + LANES] = jnp.where(lane < DIFF_HEAD_DIM, kh, 0.0).astype(dt)
        dk1_ref[:, base + LANES:base + DAUG] = ak.astype(dt)
        dk2_ref[:, base:base + LANES] = jnp.where(lane >= DIFF_HEAD_DIM, kh, 0.0).astype(dt)
        dk2_ref[:, base + LANES:base + DAUG] = ak.astype(dt)
        dva_ref[:, base:base + LANES] = dv[:, sl].astype(dt)
        dva_ref[:, base + LANES:base + DAUG] = jnp.ones((x_ref.shape[0], LANES), dt)


def _proj_weights(w_in, w_uq, w_ukv):
    d = w_in.shape[0]
    half = MLA_ROPE_DIM // 2
    edges = np.cumsum([0, Q_LORA_RANK, KV_LORA_RANK, MLA_ROPE_DIM, DIFF_WIDTH, DIFF_WIDTH, DIFF_WIDTH])
    cq, ckv, kr, dq, dk, dv = [w_in[:, edges[i]:edges[i + 1]] for i in range(6)]
    pad = jnp.zeros((d, LANES - MLA_ROPE_DIM), w_in.dtype)
    kr_sw = jnp.concatenate([kr[:, half:], kr[:, :half]], axis=1)
    wa = jnp.concatenate([cq, ckv, kr, pad, kr_sw, pad, dq, dk, dv], axis=1).astype(BF16)
    wq3 = w_uq.reshape(Q_LORA_RANK, MLA_HEADS, MLA_NOPE_DIM + MLA_ROPE_DIM)
    nope = wq3[:, :, :MLA_NOPE_DIM].reshape(Q_LORA_RANK, MLA_HEADS * MLA_NOPE_DIM)
    rp = wq3[:, :, MLA_NOPE_DIM:]
    pad3 = jnp.zeros((Q_LORA_RANK, MLA_HEADS, LANES - MLA_ROPE_DIM), w_uq.dtype)
    rope = jnp.concatenate([rp, pad3], axis=2).reshape(Q_LORA_RANK, MLA_HEADS * LANES)
    rope_sw = jnp.concatenate([rp[:, :, half:], rp[:, :, :half], pad3], axis=2).reshape(Q_LORA_RANK, MLA_HEADS * LANES)
    wq = jnp.concatenate([nope, rope, rope_sw], axis=1).astype(BF16)
    wkv3 = w_ukv.reshape(KV_LORA_RANK, MLA_HEADS, MLA_NOPE_DIM + MLA_V_DIM)
    wuk = jnp.transpose(wkv3[:, :, :MLA_NOPE_DIM], (1, 2, 0)).astype(BF16)
    wuv = jnp.transpose(wkv3[:, :, MLA_NOPE_DIM:], (1, 0, 2)).astype(BF16)
    return wa, wq, wuk, wuv


def _position_tables(pos):
    half = MLA_ROPE_DIM // 2
    inv = ROPE_BASE ** (-jnp.arange(half, dtype=F32) / half)
    ang = pos.astype(F32)[:, None] * inv
    cos, sin = jnp.cos(ang), jnp.sin(ang)
    z = jnp.zeros((pos.shape[0], LANES - MLA_ROPE_DIM), F32)
    cos2 = jnp.concatenate([cos, cos, z], axis=1)
    sin2 = jnp.concatenate([-sin, sin, z], axis=1)
    hi = (pos // POS_RADIX).astype(F32)[:, None]
    lo = (pos % POS_RADIX).astype(F32)[:, None]
    one = jnp.ones_like(hi)
    zc = jnp.zeros((pos.shape[0], LANES - 4), F32)
    ak = jnp.concatenate([hi, lo, one, one, zc], axis=1)
    aqs = []
    for h in range(DIFF_HEADS):
        s = _alibi_slope(h)
        aqs.append(jnp.concatenate([POS_RADIX * s * one, s * one, -POS_RADIX * s * hi, -s * lo, zc], axis=1))
    return cos2, sin2, jnp.concatenate(aqs, axis=1), ak


def _proj(x, pos, n_rep, attn_norm, wa, q_norm, wq, kv_norm, wuk, dt, tm):
    t, d = x.shape
    s = pos.shape[0]
    assert t == n_rep * s and s % tm == 0
    nsb = s // tm
    cos2, sin2, aq, ak = _position_tables(pos)
    row = lambda i: (i, 0)
    prow = lambda i: (i % nsb, 0)
    full = lambda i: (0, 0)
    ncol = wa.shape[1]
    outs = [
        jax.ShapeDtypeStruct((t, KV_LORA_RANK), F32),
        jax.ShapeDtypeStruct((t, MLA_ROPE_DIM), F32),
        jax.ShapeDtypeStruct((t, DIFF_WIDTH), F32),
        jax.ShapeDtypeStruct((t, DIFF_WIDTH), F32),
        jax.ShapeDtypeStruct((MLA_HEADS, t, QCAT), dt),
        jax.ShapeDtypeStruct((t, QCAT), dt),
        jax.ShapeDtypeStruct((t, DIFF_HEADS * DAUG), dt),
        jax.ShapeDtypeStruct((t, DIFF_HEADS * DAUG), dt),
        jax.ShapeDtypeStruct((t, DIFF_HEADS * DAUG), dt),
        jax.ShapeDtypeStruct((t, DIFF_HEADS * DAUG), dt),
    ]
    return pl.pallas_call(
        _proj_kernel,
        grid=(t // tm,),
        in_specs=[
            pl.BlockSpec((tm, d), row),
            pl.BlockSpec((tm, LANES), prow),
            pl.BlockSpec((tm, LANES), prow),
            pl.BlockSpec((tm, DIFF_HEADS * LANES), prow),
            pl.BlockSpec((tm, LANES), prow),
            pl.BlockSpec((1, d), full),
            pl.BlockSpec((d, ncol), full),
            pl.BlockSpec((1, Q_LORA_RANK), full),
            pl.BlockSpec(wq.shape, full),
            pl.BlockSpec((1, KV_LORA_RANK), full),
            pl.BlockSpec(wuk.shape, lambda i: (0, 0, 0)),
        ],
        out_specs=[
            pl.BlockSpec((tm, KV_LORA_RANK), row),
            pl.BlockSpec((tm, MLA_ROPE_DIM), row),
            pl.BlockSpec((tm, DIFF_WIDTH), row),
            pl.BlockSpec((tm, DIFF_WIDTH), row),
            pl.BlockSpec((MLA_HEADS, tm, QCAT), lambda i: (0, i, 0)),
            pl.BlockSpec((tm, QCAT), row),
            pl.BlockSpec((tm, DIFF_HEADS * DAUG), row),
            pl.BlockSpec((tm, DIFF_HEADS * DAUG), row),
            pl.BlockSpec((tm, DIFF_HEADS * DAUG), row),
            pl.BlockSpec((tm, DIFF_HEADS * DAUG), row),
        ],
        out_shape=outs,
        compiler_params=_cparams(("parallel",)),
        name="proj",
    )(x, cos2, sin2, aq, ak, attn_norm.reshape(1, d), wa, q_norm.reshape(1, -1), wq, kv_norm.reshape(1, -1), wuk)


def _online_update(s, v, m_ref, l_ref, acc_ref):
    m_prev = m_ref[...]
    m_new = jnp.maximum(m_prev, jnp.max(s, axis=-1, keepdims=True))
    alpha = jnp.exp(m_prev - m_new)
    p = jnp.exp(s - m_new)
    l_ref[...] = alpha * l_ref[...] + jnp.sum(p, axis=-1, keepdims=True)
    acc_ref[...] = alpha * acc_ref[...] + _dot(p.astype(BF16), v)
    m_ref[...] = m_new


def _lane_tile(x, width):
    n = width // LANES
    return x if n == 1 else jnp.concatenate([x] * n, axis=1)


def _flash_step(s, v, m_ref, l_ref, acc_ref):
    m_prev = m_ref[...]
    m_new = jnp.maximum(m_prev, jnp.max(s, axis=-1, keepdims=True))
    alpha = jnp.exp(m_prev - m_new)
    p = jnp.exp(s - _lane_tile(m_new, s.shape[1]))
    m_ref[...] = m_new
    if l_ref is not None:
        l_ref[...] = alpha * l_ref[...] + jnp.sum(p, axis=-1, keepdims=True)
    acc_ref[...] = _lane_tile(alpha, acc_ref.shape[1]) * acc_ref[...] + _dot(p.astype(BF16), v)


def _prompt_attn_kernel(qcat_ref, kvcat_ref, dqa_ref, dk1_ref, dk2_ref, dva_ref, lam_ref,
                        mla_ref, dif_ref, m_mla, l_mla, acc_mla, *dif_state, lam_init):
    m_dif = dif_state[0::2]
    acc_dif = dif_state[1::2]
    qi = pl.program_id(1)
    ki = pl.program_id(2)
    tq = dqa_ref.shape[0]
    tk = kvcat_ref.shape[0]

    @pl.when(ki == 0)
    def _():
        m_mla[...] = jnp.full(m_mla.shape, NEG_INF, F32)
        l_mla[...] = jnp.zeros(l_mla.shape, F32)
        acc_mla[...] = jnp.zeros(acc_mla.shape, F32)
        for m_ref, a_ref in zip(m_dif, acc_dif):
            m_ref[...] = jnp.full(m_ref.shape, NEG_INF, F32)
            a_ref[...] = jnp.zeros(a_ref.shape, F32)

    def step(masked):
        if masked:
            r = lax.broadcasted_iota(jnp.int32, (tq, tk), 0)
            c = lax.broadcasted_iota(jnp.int32, (tq, tk), 1)
            keep = c <= r
        kv = kvcat_ref[...]
        q = qcat_ref[...].reshape(MLA_HEADS * tq, QCAT)
        s = _dot_nt(q, kv)
        if masked:
            s = jnp.where(keep[None], s.reshape(MLA_HEADS, tq, tk), NEG_INF).reshape(MLA_HEADS * tq, tk)
        _flash_step(s, kv[:, :KV_LORA_RANK], m_mla, l_mla, acc_mla)
        for hh in range(DIFF_HEADS):
            cols = slice(hh * DAUG, (hh + 1) * DAUG)
            qa = dqa_ref[:, cols]
            v = dva_ref[:, cols]
            for mm, kref in enumerate((dk1_ref, dk2_ref)):
                s = _dot_nt(qa, kref[:, cols])
                if masked:
                    s = jnp.where(keep, s, NEG_INF)
                i = 2 * hh + mm
                _flash_step(s, v, m_dif[i], None, acc_dif[i])

    @pl.when(ki < qi)
    def _():
        step(False)

    @pl.when(ki == qi)
    def _():
        step(True)
        lam = _lambda(lam_ref, lam_init)
        for hh in range(MLA_HEADS):
            rows = slice(hh * tq, (hh + 1) * tq)
            o = acc_mla[rows, :] * _lane_tile(1.0 / l_mla[rows, :], KV_LORA_RANK)
            mla_ref[:, hh * KV_LORA_RANK:(hh + 1) * KV_LORA_RANK] = o.astype(mla_ref.dtype)
        for hh in range(DIFF_HEADS):
            a1 = acc_dif[2 * hh][...]
            a2 = acc_dif[2 * hh + 1][...]
            o1 = a1[:, :LANES] * (1.0 / a1[:, LANES:])
            o2 = a2[:, :LANES] * (1.0 / a2[:, LANES:])
            dif_ref[:, hh * LANES:(hh + 1) * LANES] = o1 - lam * o2


def _prompt_attn(qcat, kvcat, dqa, dk1, dk2, dva, lam_rows, lam_init, batch, seq, tq):
    nq = seq // tq
    qmap = lambda b, qi, ki: (b * nq + qi, 0)
    kmap = lambda b, qi, ki: (b * nq + jnp.minimum(ki, qi), 0)
    t = batch * seq
    dif_state = []
    for _ in range(2 * DIFF_HEADS):
        dif_state += [pltpu.VMEM((tq, LANES), F32), pltpu.VMEM((tq, DAUG), F32)]
    return pl.pallas_call(
        functools.partial(_prompt_attn_kernel, lam_init=lam_init),
        grid=(batch, nq, nq),
        in_specs=[
            pl.BlockSpec((MLA_HEADS, tq, QCAT), lambda b, qi, ki: (0, b * nq + qi, 0)),
            pl.BlockSpec((tq, QCAT), kmap),
            pl.BlockSpec((tq, DIFF_HEADS * DAUG), qmap),
            pl.BlockSpec((tq, DIFF_HEADS * DAUG), kmap),
            pl.BlockSpec((tq, DIFF_HEADS * DAUG), kmap),
            pl.BlockSpec((tq, DIFF_HEADS * DAUG), kmap),
            pl.BlockSpec((4, DIFF_HEAD_DIM), lambda b, qi, ki: (0, 0)),
        ],
        out_specs=[
            pl.BlockSpec((tq, MLA_HEADS * KV_LORA_RANK), qmap),
            pl.BlockSpec((tq, DIFF_WIDTH), qmap),
        ],
        out_shape=[
            jax.ShapeDtypeStruct((t, MLA_HEADS * KV_LORA_RANK), BF16),
            jax.ShapeDtypeStruct((t, DIFF_WIDTH), F32),
        ],
        scratch_shapes=[
            pltpu.VMEM((MLA_HEADS * tq, LANES), F32),
            pltpu.VMEM((MLA_HEADS * tq, LANES), F32),
            pltpu.VMEM((MLA_HEADS * tq, KV_LORA_RANK), F32),
        ] + dif_state,
        compiler_params=_cparams(("parallel", "parallel", "arbitrary")),
        name="prompt_attn",
    )(qcat, kvcat, dqa, dk1, dk2, dva, lam_rows)


def _sample_attn_kernel(pt_ref, qcat_ref, dqa_ref, nlat_ref, nkr_ref, ndk_ref, ndv_ref, lam_ref,
                        lat_hbm, kr_hbm, dk_hbm, dv_hbm,
                        mla_ref, dif_ref,
                        lat_buf, kr_buf, dk_buf, dv_buf, sems, qbd,
                        m_mla, l_mla, acc_mla, m_dif, l_dif, acc_dif,
                        *, lam_init, n_pages, chunk_pages, page, past):
    b = pl.program_id(0)
    c = pl.program_id(1)
    nb = pl.num_programs(0)
    nc = pl.num_programs(1)
    g = b * nc + c
    slot = g % 2
    dt = dqa_ref.shape[0]
    ck = chunk_pages * page
    rows_mla = MLA_HEADS * dt
    rows_dif = 2 * DIFF_HEADS * dt

    def copies(gi, sl):
        base = gi * chunk_pages
        out = []
        for j in range(chunk_pages):
            pid = pt_ref[base + j]
            hrows = pl.ds(j * page * DIFF_HEADS, page * DIFF_HEADS)
            out.append(pltpu.make_async_copy(lat_hbm.at[pid], lat_buf.at[sl, pl.ds(j * page, page)], sems.at[sl, 0]))
            out.append(pltpu.make_async_copy(kr_hbm.at[pid], kr_buf.at[sl, j], sems.at[sl, 1]))
            out.append(pltpu.make_async_copy(dk_hbm.at[pid], dk_buf.at[sl, hrows], sems.at[sl, 2]))
            out.append(pltpu.make_async_copy(dv_hbm.at[pid], dv_buf.at[sl, hrows], sems.at[sl, 3]))
        return out

    def heads_to_lanes(buf):
        return jnp.concatenate(
            [buf[pl.ds(hh, ck, stride=DIFF_HEADS), :] for hh in range(DIFF_HEADS)], axis=1).astype(BF16)

    @pl.when(g == 0)
    def _():
        for cp in copies(g, slot):
            cp.start()

    @pl.when(g + 1 < nb * nc)
    def _():
        for cp in copies(g + 1, 1 - slot):
            cp.start()

    row = lax.broadcasted_iota(jnp.int32, (rows_dif, 1), 0)
    tok_d = row % dt
    head_d = (row // dt) % DIFF_HEADS
    slope = jnp.exp2(-8.0 * (head_d.astype(F32) + 1.0) / DIFF_HEADS)

    @pl.when(c == 0)
    def _():
        m_mla[...] = jnp.full(m_mla.shape, NEG_INF, F32)
        l_mla[...] = jnp.zeros(l_mla.shape, F32)
        acc_mla[...] = jnp.zeros(acc_mla.shape, F32)
        m_dif[...] = jnp.full(m_dif.shape, NEG_INF, F32)
        l_dif[...] = jnp.zeros(l_dif.shape, F32)
        acc_dif[...] = jnp.zeros(acc_dif.shape, F32)
        dq = jnp.concatenate([dqa_ref[:, hh * DAUG:hh * DAUG + LANES] for hh in range(DIFF_HEADS)], axis=1)
        rep = jnp.broadcast_to(dq[None], (2 * DIFF_HEADS, dt, DIFF_WIDTH)).reshape(rows_dif, DIFF_WIDTH)
        col = lax.broadcasted_iota(jnp.int32, (rows_dif, DIFF_WIDTH), 1)
        own = (col // LANES == head_d) & ((col % LANES) // DIFF_HEAD_DIM == row // (DIFF_HEADS * dt))
        qbd[...] = jnp.where(own, rep, 0.0).astype(BF16)

    for cp in copies(g, slot):
        cp.wait()

    q = qcat_ref[...].reshape(rows_mla, QCAT)
    q_lat = q[:, :KV_LORA_RANK].astype(BF16)
    q_rope = q[:, KV_LORA_RANK:KV_LORA_RANK + MLA_ROPE_DIM].astype(BF16)

    lat = lat_buf[slot].astype(BF16)
    kr_t = jnp.concatenate([kr_buf[slot, j] for j in range(chunk_pages)], axis=1).astype(BF16)
    s = _dot_nt(q_lat, lat) + _dot(q_rope, kr_t)
    _online_update(s, lat, m_mla, l_mla, acc_mla)

    kd = heads_to_lanes(dk_buf.at[slot])
    vd = heads_to_lanes(dv_buf.at[slot])
    kcol = lax.broadcasted_iota(jnp.int32, (rows_dif, ck), 1)
    dist = (past + tok_d - c * ck) - kcol
    s = _dot_nt(qbd[...], kd) - slope * dist.astype(F32)
    _online_update(s, vd, m_dif, l_dif, acc_dif)

    @pl.when(c == nc - 1)
    def _():
        nlat = nlat_ref[...].astype(BF16)
        kcol_m = lax.broadcasted_iota(jnp.int32, (rows_mla, dt), 1)
        tok_m = lax.broadcasted_iota(jnp.int32, (rows_mla, dt), 0) % dt
        s = _dot_nt(q_lat, nlat) + _dot_nt(q_rope, nkr_ref[...].astype(BF16))
        s = jnp.where(kcol_m <= tok_m, s, NEG_INF)
        _online_update(s, nlat, m_mla, l_mla, acc_mla)
        kcol_n = lax.broadcasted_iota(jnp.int32, (rows_dif, dt), 1)
        s = _dot_nt(qbd[...], ndk_ref[...].astype(BF16)) - slope * (tok_d - kcol_n).astype(F32)
        s = jnp.where(kcol_n <= tok_d, s, NEG_INF)
        _online_update(s, ndv_ref[...].astype(BF16), m_dif, l_dif, acc_dif)

        lam = _lambda(lam_ref, lam_init)
        for hh in range(MLA_HEADS):
            rows = slice(hh * dt, (hh + 1) * dt)
            mla_ref[:, hh * KV_LORA_RANK:(hh + 1) * KV_LORA_RANK] = acc_mla[rows, :] * (1.0 / l_mla[rows, :])
        half = DIFF_HEADS * dt
        for hh in range(DIFF_HEADS):
            r1 = slice(hh * dt, (hh + 1) * dt)
            r2 = slice(half + hh * dt, half + (hh + 1) * dt)
            cols = slice(hh * LANES, (hh + 1) * LANES)
            o1 = acc_dif[r1, cols] * (1.0 / l_dif[r1, :])
            o2 = acc_dif[r2, cols] * (1.0 / l_dif[r2, :])
            dif_ref[:, cols] = o1 - lam * o2


def _sample_attn(page_table, qcat, dqa, nlat, nkr, ndk, ndv, lam_rows, pool_lat, pool_kr, pool_dk, pool_dv,
                 lam_init, nb, dt, chunk_pages):
    n_pages = page_table.shape[1]
    page = pool_lat.shape[1]
    assert n_pages % chunk_pages == 0
    nc = n_pages // chunk_pages
    past = n_pages * page
    t = nb * dt
    tok = lambda b, c, pt: (b, 0)
    any_spec = pl.BlockSpec(memory_space=pl.ANY)
    kern = functools.partial(_sample_attn_kernel, lam_init=lam_init, n_pages=n_pages,
                             chunk_pages=chunk_pages, page=page, past=past)
    grid_spec = pltpu.PrefetchScalarGridSpec(
        num_scalar_prefetch=1,
        grid=(nb, nc),
        in_specs=[
            pl.BlockSpec((MLA_HEADS, dt, QCAT), lambda b, c, pt: (0, b, 0)),
            pl.BlockSpec((dt, DIFF_HEADS * DAUG), tok),
            pl.BlockSpec((dt, KV_LORA_RANK), tok),
            pl.BlockSpec((dt, MLA_ROPE_DIM), tok),
            pl.BlockSpec((dt, DIFF_WIDTH), tok),
            pl.BlockSpec((dt, DIFF_WIDTH), tok),
            pl.BlockSpec((4, DIFF_HEAD_DIM), lambda b, c, pt: (0, 0)),
            any_spec, any_spec, any_spec, any_spec,
        ],
        out_specs=[
            pl.BlockSpec((dt, MLA_HEADS * KV_LORA_RANK), tok),
            pl.BlockSpec((dt, DIFF_WIDTH), tok),
        ],
        scratch_shapes=[
            pltpu.VMEM((2, chunk_pages * page, KV_LORA_RANK), F32),
            pltpu.VMEM((2, chunk_pages, MLA_ROPE_DIM, page), F32),
            pltpu.VMEM((2, chunk_pages * page * DIFF_HEADS, LANES), F32),
            pltpu.VMEM((2, chunk_pages * page * DIFF_HEADS, LANES), F32),
            pltpu.SemaphoreType.DMA((2, 4)),
            pltpu.VMEM((2 * DIFF_HEADS * dt, DIFF_WIDTH), BF16),
            pltpu.VMEM((MLA_HEADS * dt, 1), F32),
            pltpu.VMEM((MLA_HEADS * dt, 1), F32),
            pltpu.VMEM((MLA_HEADS * dt, KV_LORA_RANK), F32),
            pltpu.VMEM((2 * DIFF_HEADS * dt, 1), F32),
            pltpu.VMEM((2 * DIFF_HEADS * dt, 1), F32),
            pltpu.VMEM((2 * DIFF_HEADS * dt, DIFF_WIDTH), F32),
        ],
    )
    return pl.pallas_call(
        kern,
        grid_spec=grid_spec,
        out_shape=[
            jax.ShapeDtypeStruct((t, MLA_HEADS * KV_LORA_RANK), F32),
            jax.ShapeDtypeStruct((t, DIFF_WIDTH), F32),
        ],
        compiler_params=_cparams(("arbitrary", "arbitrary")),
        name="sample_attn",
    )(page_table.reshape(-1), qcat, dqa, nlat, nkr, ndk, ndv, lam_rows, pool_lat, pool_kr, pool_dk, pool_dv)


def _mix_out_kernel(lat_ref, dif_ref, x_ref, wuv_ref, gs_ref, wo_ref, gm_ref, wmq_ref,
                    x1_ref, qm_ref, *, lam_init):
    parts = []
    for hh in range(MLA_HEADS):
        lat = lat_ref[:, hh * KV_LORA_RANK:(hh + 1) * KV_LORA_RANK].astype(BF16)
        parts.append(_dot(lat, wuv_ref[hh]))
    gs = gs_ref[...]
    for hh in range(DIFF_HEADS):
        parts.append(_rms(dif_ref[:, hh * LANES:(hh + 1) * LANES], gs) * (1.0 - lam_init))
    mix = jnp.concatenate(parts, axis=1).astype(BF16)
    x1 = x_ref[...] + _dot(mix, wo_ref[...])
    x1_ref[...] = x1
    hq = _rms(x1, gm_ref[...]).astype(BF16)
    qm_ref[...] = (_dot(hq, wmq_ref[...]) * MEM_HEAD_DIM ** -0.5).astype(qm_ref.dtype)


def _mix_out(mla_lat, diff_o, x, wuv, subln, wo, gm, wmq, lam_init, dt, tm):
    t, d = x.shape
    row = lambda i: (i, 0)
    full = lambda i: (0, 0)
    return pl.pallas_call(
        functools.partial(_mix_out_kernel, lam_init=lam_init),
        grid=(t // tm,),
        in_specs=[
            pl.BlockSpec((tm, MLA_HEADS * KV_LORA_RANK), row),
            pl.BlockSpec((tm, DIFF_WIDTH), row),
            pl.BlockSpec((tm, d), row),
            pl.BlockSpec(wuv.shape, lambda i: (0, 0, 0)),
            pl.BlockSpec((1, LANES), full),
            pl.BlockSpec(wo.shape, full),
            pl.BlockSpec((1, d), full),
            pl.BlockSpec(wmq.shape, full),
        ],
        out_specs=[pl.BlockSpec((tm, d), row), pl.BlockSpec((tm, d), row)],
        out_shape=[jax.ShapeDtypeStruct((t, d), F32), jax.ShapeDtypeStruct((t, d), dt)],
        compiler_params=_cparams(("parallel",)),
        name="mix_out",
    )(mla_lat, diff_o, x, wuv, subln.reshape(1, -1), wo, gm.reshape(1, d), wmq)


def _mem_kv_kernel(mem_ref, g_ref, wk_ref, wv_ref, k_ref, v_ref):
    mn = _rms(mem_ref[...], g_ref[...]).astype(BF16)
    k_ref[...] = _dot(mn, wk_ref[...])
    v_ref[...] = _dot(mn, wv_ref[...])


def _mem_kv(mem, g, wk, wv, tm):
    t, d = mem.shape
    row = lambda i: (i, 0)
    full = lambda i: (0, 0)
    return pl.pallas_call(
        _mem_kv_kernel,
        grid=(t // tm,),
        in_specs=[pl.BlockSpec((tm, d), row), pl.BlockSpec((1, d), full),
                  pl.BlockSpec(wk.shape, full), pl.BlockSpec(wv.shape, full)],
        out_specs=[pl.BlockSpec((tm, d), row), pl.BlockSpec((tm, d), row)],
        out_shape=[jax.ShapeDtypeStruct((t, d), F32), jax.ShapeDtypeStruct((t, d), F32)],
        compiler_params=_cparams(("parallel",)),
        name="mem_kv",
    )(mem, g.reshape(1, d), wk, wv)


MEM_LANE_BLOCKS = MEM_HEAD_DIM // LANES


def _mem_attn_kernel(q_ref, k_ref, v_ref, o_ref, *, head_rows):
    def head(ref, hh):
        if not head_rows:
            return ref[0, :, hh * MEM_HEAD_DIM:(hh + 1) * MEM_HEAD_DIM].astype(BF16)
        per_tok = MEM_LANE_BLOCKS * MEM_HEADS
        m = ref.shape[1] // per_tok
        return jnp.concatenate(
            [ref.at[0][pl.ds(lb * MEM_HEADS + hh, m, stride=per_tok), :] for lb in range(MEM_LANE_BLOCKS)],
            axis=1).astype(BF16)

    for hh in range(MEM_HEADS):
        cols = slice(hh * MEM_HEAD_DIM, (hh + 1) * MEM_HEAD_DIM)
        q = q_ref[:, cols].astype(BF16)
        k = head(k_ref, hh)
        v = head(v_ref, hh)
        s = _dot_nt(q, k)
        p = jnp.exp(s - jnp.max(s, axis=-1, keepdims=True))
        o = _dot(p.astype(BF16), v) * (1.0 / jnp.sum(p, axis=-1, keepdims=True))
        o_ref[:, cols] = o.astype(o_ref.dtype)


def _mem_attn(qm, mk, mv, nb, rows_per_batch, tq, dt, head_rows=False):
    t, d = qm.shape
    nqb = rows_per_batch // tq
    kv_block = (1,) + mk.shape[1:]
    return pl.pallas_call(
        functools.partial(_mem_attn_kernel, head_rows=head_rows),
        grid=(nb, nqb),
        in_specs=[
            pl.BlockSpec((tq, d), lambda b, i: (b * nqb + i, 0)),
            pl.BlockSpec(kv_block, lambda b, i: (b, 0, 0)),
            pl.BlockSpec(kv_block, lambda b, i: (b, 0, 0)),
        ],
        out_specs=pl.BlockSpec((tq, d), lambda b, i: (b * nqb + i, 0)),
        out_shape=jax.ShapeDtypeStruct((t, d), dt),
        compiler_params=_cparams(("parallel", "parallel")),
        name="mem_attn",
    )(qm, mk, mv)


def _post_attn_kernel(om_ref, x1_ref, wmo_ref, gf_ref, wr_ref, br_ref, cnt_in_ref,
                      x2_ref, hn_ref, idx_ref, gate_ref, rank_ref, cnt_ref, carry):
    tm = om_ref.shape[0]

    @pl.when(pl.program_id(0) == 0)
    def _():
        carry[...] = cnt_in_ref[...]

    x2 = x1_ref[...] + _dot(om_ref[...].astype(BF16), wmo_ref[...])
    x2_ref[...] = x2
    hn = _rms(x2, gf_ref[...])
    _rows_to_tiles(hn_ref, hn)
    logits = jnp.dot(hn, wr_ref[...], preferred_element_type=F32, precision=lax.Precision.HIGHEST) + br_ref[...]
    lane_e = lax.broadcasted_iota(jnp.int32, (tm, N_EXPERTS), 1)
    lane_o = lax.broadcasted_iota(jnp.int32, (tm, LANES), 1)
    work = logits
    vals, hots = [], []
    idx_out = jnp.zeros((tm, LANES), jnp.int32)
    for k in range(TOP_K):
        mx = jnp.max(work, axis=-1, keepdims=True)
        idx = jnp.min(jnp.where(work == mx, lane_e, N_EXPERTS), axis=-1, keepdims=True)
        hot = lane_e == idx
        vals.append(mx)
        hots.append(hot)
        idx_out = jnp.where(lane_o == k, idx, idx_out)
        work = jnp.where(hot, -jnp.inf, work)
    es = [jnp.exp(v - vals[0]) for v in vals]
    den = es[0] + es[1] + es[2] + es[3]
    gate_out = jnp.zeros((tm, LANES), F32)
    for k in range(TOP_K):
        gate_out = jnp.where(lane_o == k, es[k] * (1.0 / den), gate_out)
    multi = (hots[0] | hots[1] | hots[2] | hots[3]).astype(F32)
    r = lax.broadcasted_iota(jnp.int32, (tm, tm), 0)
    cc = lax.broadcasted_iota(jnp.int32, (tm, tm), 1)
    below = (cc < r).astype(BF16)
    base = carry[...] + _dot(below, multi.astype(BF16))
    rank_out = jnp.zeros((tm, LANES), jnp.int32)
    for k in range(TOP_K):
        rk = jnp.sum(jnp.where(hots[k], base, 0.0), axis=-1, keepdims=True).astype(jnp.int32)
        rank_out = jnp.where(lane_o == k, rk, rank_out)
    idx_ref[...] = idx_out
    gate_ref[...] = gate_out
    rank_ref[...] = rank_out
    new = carry[...] + jnp.sum(multi, axis=0, keepdims=True)
    carry[...] = new
    cnt_ref[...] = new


def _post_attn(om, x1, wmo, gf, wr, br, cnt_in, tm):
    t, d = x1.shape
    row = lambda i: (i, 0)
    full = lambda i: (0, 0)
    return pl.pallas_call(
        _post_attn_kernel,
        grid=(t // tm,),
        in_specs=[
            pl.BlockSpec((tm, d), row), pl.BlockSpec((tm, d), row),
            pl.BlockSpec(wmo.shape, full), pl.BlockSpec((1, d), full),
            pl.BlockSpec(wr.shape, full), pl.BlockSpec((1, N_EXPERTS), full),
            pl.BlockSpec((1, N_EXPERTS), full),
        ],
        out_specs=[
            pl.BlockSpec((tm, d), row), pl.BlockSpec((tm * ROW_SLABS, LANES), row),
            pl.BlockSpec((tm, LANES), row), pl.BlockSpec((tm, LANES), row), pl.BlockSpec((tm, LANES), row),
            pl.BlockSpec((1, N_EXPERTS), full),
        ],
        out_shape=[
            jax.ShapeDtypeStruct((t, d), F32), jax.ShapeDtypeStruct((t * ROW_SLABS, LANES), F32),
            jax.ShapeDtypeStruct((t, LANES), jnp.int32), jax.ShapeDtypeStruct((t, LANES), F32),
            jax.ShapeDtypeStruct((t, LANES), jnp.int32),
            jax.ShapeDtypeStruct((1, N_EXPERTS), F32),
        ],
        scratch_shapes=[pltpu.VMEM((1, N_EXPERTS), F32)],
        compiler_params=_cparams(("arbitrary",)),
        name="post_attn",
    )(om, x1, wmo, gf.reshape(1, d), wr, br.reshape(1, N_EXPERTS), cnt_in)


MXU_TILE = 256
DMA_LOOP_UNROLL = 2


def _deinterleave_matrix():
    p = np.zeros((MXU_TILE, MXU_TILE), np.float32)
    half = MXU_TILE // 2
    for i in range(half):
        p[2 * i, i] = 1.0
        p[2 * i + 1, half + i] = 1.0
    return jnp.asarray(p, BF16)


def _moe_prep_kernel(w1_ref, w2_ref, perm_ref, w1g_ref, w1l_ref, w2b_ref):
    half = MXU_TILE // 2
    perm = perm_ref[...]
    for cb in range(w1_ref.shape[2] // MXU_TILE):
        blk = w1_ref[0, :, cb * MXU_TILE:(cb + 1) * MXU_TILE].astype(BF16)
        r = _dot(blk, perm)
        w1g_ref[0, :, cb * half:(cb + 1) * half] = r[:, :half].astype(BF16)
        w1l_ref[0, :, cb * half:(cb + 1) * half] = r[:, half:].astype(BF16)
    w2b_ref[...] = w2_ref[...].astype(BF16)


def _moe_prep(w1, w2):
    e, d, f2 = w1.shape
    f = f2 // 2
    blk = lambda i: (i, 0, 0)
    return pl.pallas_call(
        _moe_prep_kernel,
        grid=(e,),
        in_specs=[pl.BlockSpec((1, d, f2), blk), pl.BlockSpec((1, f, d), blk),
                  pl.BlockSpec((MXU_TILE, MXU_TILE), lambda i: (0, 0))],
        out_specs=[pl.BlockSpec((1, d, f), blk), pl.BlockSpec((1, d, f), blk), pl.BlockSpec((1, f, d), blk)],
        out_shape=[jax.ShapeDtypeStruct((e, d, f), BF16), jax.ShapeDtypeStruct((e, d, f), BF16),
                   jax.ShapeDtypeStruct((e, f, d), BF16)],
        compiler_params=_cparams(("parallel",)),
        name="moe_prep",
    )(w1, w2, _deinterleave_matrix())


def _dispatch_kernel(dest_ref, hn_ref, xs_in_ref, xs_ref, sem):
    del xs_in_ref
    tm = hn_ref.shape[0] // ROW_SLABS
    base = pl.program_id(0) * tm * TOP_K

    def copies(t):
        src = pl.multiple_of(t * ROW_SLABS, ROW_SLABS)
        out = []
        for k in range(TOP_K):
            dst = pl.multiple_of(dest_ref[base + t * TOP_K + k] * ROW_SLABS, ROW_SLABS)
            out.append(pltpu.make_async_copy(hn_ref.at[pl.ds(src, ROW_SLABS)], xs_ref.at[pl.ds(dst, ROW_SLABS)], sem))
        return out

    def start(t, carry):
        for cp in copies(t):
            cp.start()
        return carry

    def wait(t, carry):
        for cp in copies(t):
            cp.wait()
        return carry

    lax.fori_loop(0, tm, start, 0, unroll=DMA_LOOP_UNROLL)
    lax.fori_loop(0, tm, wait, 0, unroll=DMA_LOOP_UNROLL)


def _dispatch(dest_flat, hn, xs, tm):
    t = hn.shape[0] // ROW_SLABS
    grid_spec = pltpu.PrefetchScalarGridSpec(
        num_scalar_prefetch=1,
        grid=(t // tm,),
        in_specs=[pl.BlockSpec((tm * ROW_SLABS, LANES), lambda i, dest: (i, 0)), pl.BlockSpec(memory_space=pl.ANY)],
        out_specs=pl.BlockSpec(memory_space=pl.ANY),
        scratch_shapes=[pltpu.SemaphoreType.DMA(())],
    )
    return pl.pallas_call(
        _dispatch_kernel,
        grid_spec=grid_spec,
        out_shape=jax.ShapeDtypeStruct(xs.shape, xs.dtype),
        input_output_aliases={2: 0},
        compiler_params=pltpu.CompilerParams(dimension_semantics=("arbitrary",), vmem_limit_bytes=VMEM_LIMIT,
                                             has_side_effects=True),
        name="dispatch",
    )(dest_flat, hn, xs)


def _experts_kernel(te_ref, tv_ref, x_ref, w1g_ref, w1l_ref, w2_ref, b1g_ref, b1l_ref, b2_ref, y_ref):
    @pl.when(tv_ref[pl.program_id(0)] > 0)
    def _():
        x = _tiles_to_rows(x_ref, x_ref.shape[0] // ROW_SLABS).astype(BF16)
        glu = jnp.minimum(_dot(x, w1g_ref[0]) + b1g_ref[0], SWIGLU_LIMIT)
        lin = jnp.clip(_dot(x, w1l_ref[0]) + b1l_ref[0], -SWIGLU_LIMIT, SWIGLU_LIMIT)
        a = glu * jax.nn.sigmoid(SWIGLU_ALPHA * glu) * (lin + 1.0)
        _rows_to_tiles(y_ref, _dot(a.astype(BF16), w2_ref[0]) + b2_ref[0])

    @pl.when(tv_ref[pl.program_id(0)] == 0)
    def _():
        y_ref[...] = jnp.zeros(y_ref.shape, F32)


def _experts(tile_expert, tile_valid, xs, w1g, w1l, w2b, b1g, b1l, b2, tmoe):
    r = xs.shape[0] // ROW_SLABS
    d, f = w1g.shape[1:]
    wmap = lambda g, te, tv: (te[g], 0, 0)
    grid_spec = pltpu.PrefetchScalarGridSpec(
        num_scalar_prefetch=2,
        grid=(r // tmoe,),
        in_specs=[
            pl.BlockSpec((tmoe * ROW_SLABS, LANES), lambda g, te, tv: (g, 0)),
            pl.BlockSpec((1, d, f), wmap), pl.BlockSpec((1, d, f), wmap), pl.BlockSpec((1, f, d), wmap),
            pl.BlockSpec((1, 1, f), wmap), pl.BlockSpec((1, 1, f), wmap), pl.BlockSpec((1, 1, d), wmap),
        ],
        out_specs=pl.BlockSpec((tmoe * ROW_SLABS, LANES), lambda g, te, tv: (g, 0)),
    )
    return pl.pallas_call(
        _experts_kernel,
        grid_spec=grid_spec,
        out_shape=jax.ShapeDtypeStruct(xs.shape, F32),
        compiler_params=_cparams(("arbitrary",)),
        name="moe_experts",
    )(tile_expert, tile_valid, xs, w1g, w1l, w2b, b1g, b1l, b2)


def _combine_kernel(dest_ref, x2_ref, gate_ref, gf_ref, ys_hbm, y_ref, buf, sems):
    tm = x2_ref.shape[0]
    i = pl.program_id(0)
    slot = i % 2

    def copies(tile, sl, t):
        dst = pl.multiple_of(t * ROW_SLABS, ROW_SLABS)
        out = []
        for k in range(TOP_K):
            src = pl.multiple_of(dest_ref[(tile * tm + t) * TOP_K + k] * ROW_SLABS, ROW_SLABS)
            out.append(pltpu.make_async_copy(ys_hbm.at[pl.ds(src, ROW_SLABS)],
                                             buf.at[sl, k, pl.ds(dst, ROW_SLABS)], sems.at[sl]))
        return out

    def start_tile(tile, sl):
        def body(t, carry):
            for cp in copies(tile, sl, t):
                cp.start()
            return carry
        lax.fori_loop(0, tm, body, 0, unroll=DMA_LOOP_UNROLL)

    @pl.when(i == 0)
    def _():
        start_tile(i, slot)

    @pl.when(i + 1 < pl.num_programs(0))
    def _():
        start_tile(i + 1, 1 - slot)

    def wait(t, carry):
        for cp in copies(i, slot, t):
            cp.wait()
        return carry

    lax.fori_loop(0, tm, wait, 0, unroll=DMA_LOOP_UNROLL)
    gate = gate_ref[...]
    parts = []
    for s in range(ROW_SLABS):
        acc = x2_ref[:, s * LANES:(s + 1) * LANES]
        for k in range(TOP_K):
            acc = acc + gate[:, k:k + 1] * buf.at[slot, k][pl.ds(s, tm, stride=ROW_SLABS), :]
        parts.append(acc)
    y_ref[...] = _rms(jnp.concatenate(parts, axis=1), gf_ref[...])


def _combine(dest_flat, x2, gate, final_norm, ys, tm):
    t, d = x2.shape
    grid_spec = pltpu.PrefetchScalarGridSpec(
        num_scalar_prefetch=1,
        grid=(t // tm,),
        in_specs=[
            pl.BlockSpec((tm, d), lambda i, dest: (i, 0)),
            pl.BlockSpec((tm, LANES), lambda i, dest: (i, 0)),
            pl.BlockSpec((1, d), lambda i, dest: (0, 0)),
            pl.BlockSpec(memory_space=pl.ANY),
        ],
        out_specs=pl.BlockSpec((tm, d), lambda i, dest: (i, 0)),
        scratch_shapes=[pltpu.VMEM((2, TOP_K, tm * ROW_SLABS, LANES), F32), pltpu.SemaphoreType.DMA((2,))],
    )
    return pl.pallas_call(
        _combine_kernel,
        grid_spec=grid_spec,
        out_shape=jax.ShapeDtypeStruct((t, d), F32),
        compiler_params=_cparams(("arbitrary",)),
        name="combine",
    )(dest_flat, x2, gate, final_norm.reshape(1, d), ys)


def _slot_plan(counts, n_tiles, tmoe):
    counts = counts.reshape(-1).astype(jnp.int32)
    tiles = (counts + tmoe - 1) // tmoe
    ends = jnp.cumsum(tiles)
    row_start = (ends - tiles) * tmoe
    g = jnp.arange(n_tiles, dtype=jnp.int32)
    total = ends[-1]
    last_expert = jnp.max(jnp.where(counts > 0, jnp.arange(N_EXPERTS, dtype=jnp.int32), 0))
    te = jnp.sum((g[:, None] >= ends[None, :]).astype(jnp.int32), axis=1)
    tile_expert = jnp.where(g < total, jnp.minimum(te, N_EXPERTS - 1), last_expert).astype(jnp.int32)
    tile_valid = (g < total).astype(jnp.int32)
    return row_start, tile_expert, tile_valid


TM_TOKENS = 256
TQ_ATTN = 512
TM_ROUTE = 256
TM_EXPERT = 256
CHUNK_PAGES = 16


def kernel(x_prompt, x_sample, cache_mla_latent, cache_mla_krope, cache_diff_k, cache_diff_v, cache_mem_k, cache_mem_v, page_table, mem_prompt, attn_norm, w_in, q_norm, w_uq, kv_norm, w_ukv, lambda_q1, lambda_k1, lambda_q2, lambda_k2, subln_norm, w_o, mem_q_norm, mem_kv_norm, w_mq, w_mk, w_mv, w_mo, ffn_norm, w_router, b_router, w_mlp1, b_mlp1, w_mlp2, b_mlp2, final_norm):
    depth = w_in.shape[0]
    assert depth == 1
    for h in range(DIFF_HEADS):
        assert math.frexp(_alibi_slope(h))[0] == 0.5
    l = 0
    lam_init = 0.8 - 0.6 * math.exp(-0.3 * l)
    nbp, seq, d = x_prompt.shape
    nbs, dseq, _ = x_sample.shape
    n_phys, page = cache_mla_latent.shape[1:3]
    n_pages = page_table.shape[1]
    past = n_pages * page
    mem_len = mem_prompt.shape[1]
    assert seq <= POS_RADIX * POS_RADIX and page == LANES and d == D_MODEL
    tp, ts = nbp * seq, nbs * dseq

    wa, wq, wuk, wuv = _proj_weights(w_in[l], w_uq[l], w_ukv[l])
    lam_rows = jnp.stack([lambda_q1[l], lambda_k1[l], lambda_q2[l], lambda_k2[l]]).astype(F32)
    wo = w_o[l].astype(BF16)
    wmq = w_mq[l].astype(BF16)
    wmo = w_mo[l].astype(BF16)

    xp = x_prompt.reshape(tp, d)
    tm_p = min(TM_TOKENS, seq)
    (p_ckv, p_kr, p_dk, p_dv, p_qcat, p_kvcat, p_dqa, p_dk1, p_dk2, p_dva) = _proj(
        xp, jnp.arange(seq, dtype=jnp.int32), nbp, attn_norm[l], wa, q_norm[l], wq, kv_norm[l], wuk, BF16, tm_p)
    p_lat, p_dif = _prompt_attn(p_qcat, p_kvcat, p_dqa, p_dk1, p_dk2, p_dva, lam_rows, lam_init,
                                nbp, seq, min(TQ_ATTN, seq))
    p_x1, p_qm = _mix_out(p_lat, p_dif, xp, wuv, subln_norm[l], wo, mem_q_norm[l], wmq, lam_init, BF16, tm_p)
    mk, mv = _mem_kv(mem_prompt.reshape(nbp * mem_len, d), mem_kv_norm[l], w_mk[l].astype(BF16),
                     w_mv[l].astype(BF16), min(TM_TOKENS, nbp * mem_len))
    p_om = _mem_attn(p_qm, mk.reshape(nbp, mem_len, d), mv.reshape(nbp, mem_len, d), nbp, seq, tm_p, BF16)

    xs_tok = x_sample.reshape(ts, d)
    tm_s = min(TM_TOKENS, ts)
    pos_s = past + jnp.arange(dseq, dtype=jnp.int32)
    (s_ckv, s_kr, s_dk, s_dv, s_qcat, _, s_dqa, _, _, _) = _proj(
        xs_tok, jnp.tile(pos_s, tm_s // dseq), ts // tm_s, attn_norm[l], wa, q_norm[l], wq, kv_norm[l], wuk,
        F32, tm_s)
    s_lat, s_dif = _sample_attn(
        page_table, s_qcat, s_dqa, s_ckv, s_kr, s_dk, s_dv, lam_rows,
        cache_mla_latent.reshape(n_phys, page, KV_LORA_RANK),
        jnp.swapaxes(cache_mla_krope.reshape(n_phys, page, MLA_ROPE_DIM), 1, 2),
        cache_diff_k.reshape(n_phys, page * DIFF_HEADS, LANES), cache_diff_v.reshape(n_phys, page * DIFF_HEADS, LANES),
        lam_init, nbs, dseq, min(CHUNK_PAGES, n_pages))
    s_x1, s_qm = _mix_out(s_lat, s_dif, xs_tok, wuv, subln_norm[l], wo, mem_q_norm[l], wmq, lam_init, F32, tm_s)
    def mem_head_rows(c):
        c = c.reshape(nbs, mem_len, MEM_HEADS, MEM_LANE_BLOCKS, LANES)
        return jnp.transpose(c, (0, 1, 3, 2, 4)).reshape(nbs, mem_len * MEM_LANE_BLOCKS * MEM_HEADS, LANES)

    s_om = _mem_attn(s_qm, mem_head_rows(cache_mem_k), mem_head_rows(cache_mem_v), nbs, dseq, dseq, F32,
                     head_rows=True)

    tr_p, tr_s = min(TM_ROUTE, tp), min(TM_ROUTE, ts)
    zero_cnt = jnp.zeros((1, N_EXPERTS), F32)
    p_x2, p_hn, p_idx, p_gate, p_rank, cnt_p = _post_attn(p_om, p_x1, wmo, ffn_norm[l], w_router[l], b_router[l],
                                                          zero_cnt, tr_p)
    s_x2, s_hn, s_idx, s_gate, s_rank, cnt = _post_attn(s_om, s_x1, wmo, ffn_norm[l], w_router[l], b_router[l],
                                                        cnt_p, tr_s)
    n_pairs = (tp + ts) * TOP_K
    n_tiles = (n_pairs + N_EXPERTS * (TM_EXPERT - 1)) // TM_EXPERT + 1
    row_start, tile_expert, tile_valid = _slot_plan(cnt, n_tiles, TM_EXPERT)
    p_dest = (row_start[p_idx[:, :TOP_K]] + p_rank[:, :TOP_K]).reshape(-1)
    s_dest = (row_start[s_idx[:, :TOP_K]] + s_rank[:, :TOP_K]).reshape(-1)

    w1g, w1l, w2b = _moe_prep(w_mlp1.reshape(w_mlp1.shape[1:]), w_mlp2.reshape(w_mlp2.shape[1:]))
    b1 = b_mlp1[l].reshape(N_EXPERTS, D_FF, 2)
    b1g = b1[:, :, 0].reshape(N_EXPERTS, 1, D_FF)
    b1l = b1[:, :, 1].reshape(N_EXPERTS, 1, D_FF)
    b2 = b_mlp2[l].reshape(N_EXPERTS, 1, d)
    slots = jnp.zeros((n_tiles * TM_EXPERT * ROW_SLABS, LANES), F32)
    slots = _dispatch(p_dest, p_hn, slots, tr_p)
    slots = _dispatch(s_dest, s_hn, slots, tr_s)
    ys = _experts(tile_expert, tile_valid, slots, w1g, w1l, w2b, b1g, b1l, b2, TM_EXPERT)
    y_p = _combine(p_dest, p_x2, p_gate, final_norm, ys, tr_p)
    y_s = _combine(s_dest, s_x2, s_gate, final_norm, ys, tr_s)

    dshape = (DIFF_HEADS, 2 * DIFF_HEAD_DIM)
    mshape = (MEM_HEADS, MEM_HEAD_DIM)
    return (
        y_p.reshape(nbp, seq, d),
        y_s.reshape(nbs, dseq, d),
        p_ckv.reshape(1, nbp, seq, KV_LORA_RANK),
        p_kr.reshape(1, nbp, seq, MLA_ROPE_DIM),
        p_dk.reshape(1, nbp, seq, *dshape),
        p_dv.reshape(1, nbp, seq, *dshape),
        mk.reshape(1, nbp, mem_len, *mshape),
        mv.reshape(1, nbp, mem_len, *mshape),
        s_ckv.reshape(1, nbs, dseq, KV_LORA_RANK),
        s_kr.reshape(1, nbs, dseq, MLA_ROPE_DIM),
        s_dk.reshape(1, nbs, dseq, *dshape),
        s_dv.reshape(1, nbs, dseq, *dshape),
    )
```

```python
import functools
import math

import numpy as np
import jax
import jax.numpy as jnp
from jax import lax
from jax.experimental import pallas as pl
from jax.experimental.pallas import tpu as pltpu

D_MODEL = 1024
MLA_HEADS = 4
MLA_NOPE_DIM = 128
MLA_ROPE_DIM = 64
MLA_V_DIM = 128
Q_LORA_RANK = 384
KV_LORA_RANK = 256
ROPE_BASE = 10000.0
MLA_SCALE = (MLA_NOPE_DIM + MLA_ROPE_DIM) ** -0.5
DIFF_HEADS = 4
DIFF_HEAD_DIM = 64
DIFF_SCALE = DIFF_HEAD_DIM ** -0.5
DIFF_WIDTH = DIFF_HEADS * 2 * DIFF_HEAD_DIM
MEM_HEADS = 4
MEM_HEAD_DIM = D_MODEL // MEM_HEADS
N_EXPERTS = 32
TOP_K = 4
D_FF = D_MODEL
SWIGLU_ALPHA = 1.702
SWIGLU_LIMIT = 7.0
NORM_EPS = 1e-6
NEG_INF = -1e30

LANES = 128
QCAT = KV_LORA_RANK + LANES
DAUG = 2 * LANES
POS_RADIX = 64
VMEM_LIMIT = 56 * 1024 * 1024

BF16 = jnp.bfloat16
F32 = jnp.float32


def _cparams(sem):
    return pltpu.CompilerParams(dimension_semantics=sem, vmem_limit_bytes=VMEM_LIMIT)


def _rms(x, g):
    return x * lax.rsqrt(jnp.mean(x * x, axis=-1, keepdims=True) + NORM_EPS) * g


def _dot(a, b):
    return jnp.dot(a, b, preferred_element_type=F32)


def _dot_nt(a, b):
    return lax.dot_general(a, b, (((1,), (1,)), ((), ())), preferred_element_type=F32)


ROW_SLABS = D_MODEL // LANES


def _rows_to_tiles(ref, x):
    n = x.shape[0]
    for s in range(ROW_SLABS):
        ref[pl.ds(s, n, stride=ROW_SLABS), :] = x[:, s * LANES:(s + 1) * LANES]


def _tiles_to_rows(ref, n):
    return jnp.concatenate([ref[pl.ds(s, n, stride=ROW_SLABS), :] for s in range(ROW_SLABS)], axis=1)


def _alibi_slope(h):
    return 2.0 ** (-8.0 * (h + 1.0) / DIFF_HEADS)


def _lambda(lam_ref, lam_init):
    v = lam_ref[...]
    a = jnp.sum(v[0:1] * v[1:2], axis=-1, keepdims=True)
    b = jnp.sum(v[2:3] * v[3:4], axis=-1, keepdims=True)
    return jnp.exp(a) - jnp.exp(b) + lam_init


def _proj_kernel(x_ref, cos_ref, sin_ref, aq_ref, ak_ref, ga_ref, wa_ref, gq_ref, wq_ref, gkv_ref, wuk_ref,
                 ckv_ref, kr_ref, dk_ref, dv_ref, qcat_ref, kvcat_ref, dqa_ref, dk1_ref, dk2_ref, dva_ref):
    dt = qcat_ref.dtype
    h = _rms(x_ref[...], ga_ref[...])
    u = _dot(h.astype(BF16), wa_ref[...])
    o = 0
    cq = u[:, o:o + Q_LORA_RANK]; o += Q_LORA_RANK
    ckv = u[:, o:o + KV_LORA_RANK]; o += KV_LORA_RANK
    krp = u[:, o:o + LANES]; o += LANES
    krs = u[:, o:o + LANES]; o += LANES
    dq = u[:, o:o + DIFF_WIDTH]; o += DIFF_WIDTH
    dk = u[:, o:o + DIFF_WIDTH]; o += DIFF_WIDTH
    dv = u[:, o:o + DIFF_WIDTH]
    cos = cos_ref[...]
    sin = sin_ref[...]
    k_rope = krp * cos + krs * sin
    c_kv = _rms(ckv, gkv_ref[...])
    ckv_ref[...] = c_kv
    kr_ref[...] = k_rope[:, :MLA_ROPE_DIM]
    dk_ref[...] = dk
    dv_ref[...] = dv
    kvcat_ref[:, :KV_LORA_RANK] = c_kv.astype(dt)
    kvcat_ref[:, KV_LORA_RANK:] = k_rope.astype(dt)
    q = _dot(_rms(cq, gq_ref[...]).astype(BF16), wq_ref[...])
    nq = MLA_HEADS * LANES
    lane = lax.broadcasted_iota(jnp.int32, (x_ref.shape[0], LANES), 1)
    ak = ak_ref[...]
    for hh in range(MLA_HEADS):
        sl = slice(hh * LANES, (hh + 1) * LANES)
        q_lat = _dot(q[:, sl].astype(BF16), wuk_ref[hh]) * MLA_SCALE
        q_rope = (q[:, nq + hh * LANES:nq + (hh + 1) * LANES] * cos
                  + q[:, 2 * nq + hh * LANES:2 * nq + (hh + 1) * LANES] * sin) * MLA_SCALE
        qcat_ref[hh, :, :KV_LORA_RANK] = q_lat.astype(dt)
        qcat_ref[hh, :, KV_LORA_RANK:] = q_rope.astype(dt)
    for hh in range(DIFF_HEADS):
        sl = slice(hh * LANES, (hh + 1) * LANES)
        base = hh * DAUG
        dqa_ref[:, base:base + LANES] = (dq[:, sl] * DIFF_SCALE).astype(dt)
        dqa_ref[:, base + LANES:base + DAUG] = aq_ref[:, sl].astype(dt)
        kh = dk[:, sl]
        dk1_ref[:, base:base + LANES] = jnp.where(lane < DIFF_HEAD_DIM, kh, 0.0).astype(dt)
        dk1_ref[:, base + LANES:base + DAUG] = ak.astype(dt)
        dk2_ref[:, base:base + LANES] = jnp.where(lane >= DIFF_HEAD_DIM, kh, 0.0).astype(dt)
        dk2_ref[:, base + LANES:base + DAUG] = ak.astype(dt)
        dva_ref[:, base:base + LANES] = dv[:, sl].astype(dt)
        dva_ref[:, base + LANES:base + DAUG] = jnp.ones((x_ref.shape[0], LANES), dt)


def _proj_weights(w_in, w_uq, w_ukv):
    d = w_in.shape[0]
    half = MLA_ROPE_DIM // 2
    edges = np.cumsum([0, Q_LORA_RANK, KV_LORA_RANK, MLA_ROPE_DIM, DIFF_WIDTH, DIFF_WIDTH, DIFF_WIDTH])
    cq, ckv, kr, dq, dk, dv = [w_in[:, edges[i]:edges[i + 1]] for i in range(6)]
    pad = jnp.zeros((d, LANES - MLA_ROPE_DIM), w_in.dtype)
    kr_sw = jnp.concatenate([kr[:, half:], kr[:, :half]], axis=1)
    wa = jnp.concatenate([cq, ckv, kr, pad, kr_sw, pad, dq, dk, dv], axis=1).astype(BF16)
    wq3 = w_uq.reshape(Q_LORA_RANK, MLA_HEADS, MLA_NOPE_DIM + MLA_ROPE_DIM)
    nope = wq3[:, :, :MLA_NOPE_DIM].reshape(Q_LORA_RANK, MLA_HEADS * MLA_NOPE_DIM)
    rp = wq3[:, :, MLA_NOPE_DIM:]
    pad3 = jnp.zeros((Q_LORA_RANK, MLA_HEADS, LANES - MLA_ROPE_DIM), w_uq.dtype)
    rope = jnp.concatenate([rp, pad3], axis=2).reshape(Q_LORA_RANK, MLA_HEADS * LANES)
    rope_sw = jnp.concatenate([rp[:, :, half:], rp[:, :, :half], pad3], axis=2).reshape(Q_LORA_RANK, MLA_HEADS * LANES)
    wq = jnp.concatenate([nope, rope, rope_sw], axis=1).astype(BF16)
    wkv3 = w_ukv.reshape(KV_LORA_RANK, MLA_HEADS, MLA_NOPE_DIM + MLA_V_DIM)
    wuk = jnp.transpose(wkv3[:, :, :MLA_NOPE_DIM], (1, 2, 0)).astype(BF16)
    wuv = jnp.transpose(wkv3[:, :, MLA_NOPE_DIM:], (1, 0, 2)).astype(BF16)
    return wa, wq, wuk, wuv


def _position_tables(pos):
    half = MLA_ROPE_DIM // 2
    inv = ROPE_BASE ** (-jnp.arange(half, dtype=F32) / half)
    ang = pos.astype(F32)[:, None] * inv
    cos, sin = jnp.cos(ang), jnp.sin(ang)
    z = jnp.zeros((pos.shape[0], LANES - MLA_ROPE_DIM), F32)
    cos2 = jnp.concatenate([cos, cos, z], axis=1)
    sin2 = jnp.concatenate([-sin, sin, z], axis=1)
    hi = (pos // POS_RADIX).astype(F32)[:, None]
    lo = (pos % POS_RADIX).astype(F32)[:, None]
    one = jnp.ones_like(hi)
    zc = jnp.zeros((pos.shape[0], LANES - 4), F32)
    ak = jnp.concatenate([hi, lo, one, one, zc], axis=1)
    aqs = []
    for h in range(DIFF_HEADS):
        s = _alibi_slope(h)
        aqs.append(jnp.concatenate([POS_RADIX * s * one, s * one, -POS_RADIX * s * hi, -s * lo, zc], axis=1))
    return cos2, sin2, jnp.concatenate(aqs, axis=1), ak


def _proj(x, pos, n_rep, attn_norm, wa, q_norm, wq, kv_norm, wuk, dt, tm):
    t, d = x.shape
    s = pos.shape[0]
    assert t == n_rep * s and s % tm == 0
    nsb = s // tm
    cos2, sin2, aq, ak = _position_tables(pos)
    row = lambda i: (i, 0)
    prow = lambda i: (i % nsb, 0)
    full = lambda i: (0, 0)
    ncol = wa.shape[1]
    outs = [
        jax.ShapeDtypeStruct((t, KV_LORA_RANK), F32),
        jax.ShapeDtypeStruct((t, MLA_ROPE_DIM), F32),
        jax.ShapeDtypeStruct((t, DIFF_WIDTH), F32),
        jax.ShapeDtypeStruct((t, DIFF_WIDTH), F32),
        jax.ShapeDtypeStruct((MLA_HEADS, t, QCAT), dt),
        jax.ShapeDtypeStruct((t, QCAT), dt),
        jax.ShapeDtypeStruct((t, DIFF_HEADS * DAUG), dt),
        jax.ShapeDtypeStruct((t, DIFF_HEADS * DAUG), dt),
        jax.ShapeDtypeStruct((t, DIFF_HEADS * DAUG), dt),
        jax.ShapeDtypeStruct((t, DIFF_HEADS * DAUG), dt),
    ]
    return pl.pallas_call(
        _proj_kernel,
        grid=(t // tm,),
        in_specs=[
            pl.BlockSpec((tm, d), row),
            pl.BlockSpec((tm, LANES), prow),
            pl.BlockSpec((tm, LANES), prow),
            pl.BlockSpec((tm, DIFF_HEADS * LANES), prow),
            pl.BlockSpec((tm, LANES), prow),
            pl.BlockSpec((1, d), full),
            pl.BlockSpec((d, ncol), full),
            pl.BlockSpec((1, Q_LORA_RANK), full),
            pl.BlockSpec(wq.shape, full),
            pl.BlockSpec((1, KV_LORA_RANK), full),
            pl.BlockSpec(wuk.shape, lambda i: (0, 0, 0)),
        ],
        out_specs=[
            pl.BlockSpec((tm, KV_LORA_RANK), row),
            pl.BlockSpec((tm, MLA_ROPE_DIM), row),
            pl.BlockSpec((tm, DIFF_WIDTH), row),
            pl.BlockSpec((tm, DIFF_WIDTH), row),
            pl.BlockSpec((MLA_HEADS, tm, QCAT), lambda i: (0, i, 0)),
            pl.BlockSpec((tm, QCAT), row),
            pl.BlockSpec((tm, DIFF_HEADS * DAUG), row),
            pl.BlockSpec((tm, DIFF_HEADS * DAUG), row),
            pl.BlockSpec((tm, DIFF_HEADS * DAUG), row),
            pl.BlockSpec((tm, DIFF_HEADS * DAUG), row),
        ],
        out_shape=outs,
        compiler_params=_cparams(("parallel",)),
        name="proj",
    )(x, cos2, sin2, aq, ak, attn_norm.reshape(1, d), wa, q_norm.reshape(1, -1), wq, kv_norm.reshape(1, -1), wuk)


def _online_update(s, v, m_ref, l_ref, acc_ref):
    m_prev = m_ref[...]
    m_new = jnp.maximum(m_prev, jnp.max(s, axis=-1, keepdims=True))
    alpha = jnp.exp(m_prev - m_new)
    p = jnp.exp(s - m_new)
    l_ref[...] = alpha * l_ref[...] + jnp.sum(p, axis=-1, keepdims=True)
    acc_ref[...] = alpha * acc_ref[...] + _dot(p.astype(BF16), v)
    m_ref[...] = m_new


def _lane_tile(x, width):
    n = width // LANES
    return x if n == 1 else jnp.concatenate([x] * n, axis=1)


def _flash_step(s, v, m_ref, l_ref, acc_ref):
    m_prev = m_ref[...]
    m_new = jnp.maximum(m_prev, jnp.max(s, axis=-1, keepdims=True))
    alpha = jnp.exp(m_prev - m_new)
    p = jnp.exp(s - _lane_tile(m_new, s.shape[1]))
    m_ref[...] = m_new
    if l_ref is not None:
        l_ref[...] = alpha * l_ref[...] + jnp.sum(p, axis=-1, keepdims=True)
    acc_ref[...] = _lane_tile(alpha, acc_ref.shape[1]) * acc_ref[...] + _dot(p.astype(BF16), v)


def _prompt_attn_kernel(qcat_ref, kvcat_ref, dqa_ref, dk1_ref, dk2_ref, dva_ref, lam_ref,
                        mla_ref, dif_ref, m_mla, l_mla, acc_mla, *dif_state, lam_init):
    m_dif = dif_state[0::2]
    acc_dif = dif_state[1::2]
    qi = pl.program_id(1)
    ki = pl.program_id(2)
    tq = dqa_ref.shape[0]
    tk = kvcat_ref.shape[0]

    @pl.when(ki == 0)
    def _():
        m_mla[...] = jnp.full(m_mla.shape, NEG_INF, F32)
        l_mla[...] = jnp.zeros(l_mla.shape, F32)
        acc_mla[...] = jnp.zeros(acc_mla.shape, F32)
        for m_ref, a_ref in zip(m_dif, acc_dif):
            m_ref[...] = jnp.full(m_ref.shape, NEG_INF, F32)
            a_ref[...] = jnp.zeros(a_ref.shape, F32)

    def step(masked):
        if masked:
            r = lax.broadcasted_iota(jnp.int32, (tq, tk), 0)
            c = lax.broadcasted_iota(jnp.int32, (tq, tk), 1)
            keep = c <= r
        kv = kvcat_ref[...]
        q = qcat_ref[...].reshape(MLA_HEADS * tq, QCAT)
        s = _dot_nt(q, kv)
        if masked:
            s = jnp.where(keep[None], s.reshape(MLA_HEADS, tq, tk), NEG_INF).reshape(MLA_HEADS * tq, tk)
        _flash_step(s, kv[:, :KV_LORA_RANK], m_mla, l_mla, acc_mla)
        for hh in range(DIFF_HEADS):
            cols = slice(hh * DAUG, (hh + 1) * DAUG)
            qa = dqa_ref[:, cols]
            v = dva_ref[:, cols]
            for mm, kref in enumerate((dk1_ref, dk2_ref)):
                s = _dot_nt(qa, kref[:, cols])
                if masked:
                    s = jnp.where(keep, s, NEG_INF)
                i = 2 * hh + mm
                _flash_step(s, v, m_dif[i], None, acc_dif[i])

    @pl.when(ki < qi)
    def _():
        step(False)

    @pl.when(ki == qi)
    def _():
        step(True)
        lam = _lambda(lam_ref, lam_init)
        for hh in range(MLA_HEADS):
            rows = slice(hh * tq, (hh + 1) * tq)
            o = acc_mla[rows, :] * _lane_tile(1.0 / l_mla[rows, :], KV_LORA_RANK)
            mla_ref[:, hh * KV_LORA_RANK:(hh + 1) * KV_LORA_RANK] = o.astype(mla_ref.dtype)
        for hh in range(DIFF_HEADS):
            a1 = acc_dif[2 * hh][...]
            a2 = acc_dif[2 * hh + 1][...]
            o1 = a1[:, :LANES] * (1.0 / a1[:, LANES:])
            o2 = a2[:, :LANES] * (1.0 / a2[:, LANES:])
            dif_ref[:, hh * LANES:(hh + 1) * LANES] = o1 - lam * o2


def _prompt_attn(qcat, kvcat, dqa, dk1, dk2, dva, lam_rows, lam_init, batch, seq, tq):
    nq = seq // tq
    qmap = lambda b, qi, ki: (b * nq + qi, 0)
    kmap = lambda b, qi, ki: (b * nq + jnp.minimum(ki, qi), 0)
    t = batch * seq
    dif_state = []
    for _ in range(2 * DIFF_HEADS):
        dif_state += [pltpu.VMEM((tq, LANES), F32), pltpu.VMEM((tq, DAUG), F32)]
    return pl.pallas_call(
        functools.partial(_prompt_attn_kernel, lam_init=lam_init),
        grid=(batch, nq, nq),
        in_specs=[
            pl.BlockSpec((MLA_HEADS, tq, QCAT), lambda b, qi, ki: (0, b * nq + qi, 0)),
            pl.BlockSpec((tq, QCAT), kmap),
            pl.BlockSpec((tq, DIFF_HEADS * DAUG), qmap),
            pl.BlockSpec((tq, DIFF_HEADS * DAUG), kmap),
            pl.BlockSpec((tq, DIFF_HEADS * DAUG), kmap),
            pl.BlockSpec((tq, DIFF_HEADS * DAUG), kmap),
            pl.BlockSpec((4, DIFF_HEAD_DIM), lambda b, qi, ki: (0, 0)),
        ],
        out_specs=[
            pl.BlockSpec((tq, MLA_HEADS * KV_LORA_RANK), qmap),
            pl.BlockSpec((tq, DIFF_WIDTH), qmap),
        ],
        out_shape=[
            jax.ShapeDtypeStruct((t, MLA_HEADS * KV_LORA_RANK), BF16),
            jax.ShapeDtypeStruct((t, DIFF_WIDTH), F32),
        ],
        scratch_shapes=[
            pltpu.VMEM((MLA_HEADS * tq, LANES), F32),
            pltpu.VMEM((MLA_HEADS * tq, LANES), F32),
            pltpu.VMEM((MLA_HEADS * tq, KV_LORA_RANK), F32),
        ] + dif_state,
        compiler_params=_cparams(("parallel", "parallel", "arbitrary")),
        name="prompt_attn",
    )(qcat, kvcat, dqa, dk1, dk2, dva, lam_rows)


def _sample_attn_kernel(pt_ref, qcat_ref, dqa_ref, nlat_ref, nkr_ref, ndk_ref, ndv_ref, lam_ref,
                        lat_hbm, kr_hbm, dk_hbm, dv_hbm,
                        mla_ref, dif_ref,
                        lat_buf, kr_buf, dk_buf, dv_buf, sems, qbd,
                        m_mla, l_mla, acc_mla, m_dif, l_dif, acc_dif,
                        *, lam_init, n_pages, chunk_pages, page, past):
    b = pl.program_id(0)
    c = pl.program_id(1)
    nb = pl.num_programs(0)
    nc = pl.num_programs(1)
    g = b * nc + c
    slot = g % 2
    dt = dqa_ref.shape[0]
    ck = chunk_pages * page
    rows_mla = MLA_HEADS * dt
    rows_dif = 2 * DIFF_HEADS * dt

    def copies(gi, sl):
        base = gi * chunk_pages
        out = []
        for j in range(chunk_pages):
            pid = pt_ref[base + j]
            hrows = pl.ds(j * page * DIFF_HEADS, page * DIFF_HEADS)
            out.append(pltpu.make_async_copy(lat_hbm.at[pid], lat_buf.at[sl, pl.ds(j * page, page)], sems.at[sl, 0]))
            out.append(pltpu.make_async_copy(kr_hbm.at[pid], kr_buf.at[sl, j], sems.at[sl, 1]))
            out.append(pltpu.make_async_copy(dk_hbm.at[pid], dk_buf.at[sl, hrows], sems.at[sl, 2]))
            out.append(pltpu.make_async_copy(dv_hbm.at[pid], dv_buf.at[sl, hrows], sems.at[sl, 3]))
        return out

    def heads_to_lanes(buf):
        return jnp.concatenate(
            [buf[pl.ds(hh, ck, stride=DIFF_HEADS), :] for hh in range(DIFF_HEADS)], axis=1).astype(BF16)

    @pl.when(g == 0)
    def _():
        for cp in copies(g, slot):
            cp.start()

    @pl.when(g + 1 < nb * nc)
    def _():
        for cp in copies(g + 1, 1 - slot):
            cp.start()

    row = lax.broadcasted_iota(jnp.int32, (rows_dif, 1), 0)
    tok_d = row % dt
    head_d = (row // dt) % DIFF_HEADS
    slope = jnp.exp2(-8.0 * (head_d.astype(F32) + 1.0) / DIFF_HEADS)

    @pl.when(c == 0)
    def _():
        m_mla[...] = jnp.full(m_mla.shape, NEG_INF, F32)
        l_mla[...] = jnp.zeros(l_mla.shape, F32)
        acc_mla[...] = jnp.zeros(acc_mla.shape, F32)
        m_dif[...] = jnp.full(m_dif.shape, NEG_INF, F32)
        l_dif[...] = jnp.zeros(l_dif.shape, F32)
        acc_dif[...] = jnp.zeros(acc_dif.shape, F32)
        dq = jnp.concatenate([dqa_ref[:, hh * DAUG:hh * DAUG + LANES] for hh in range(DIFF_HEADS)], axis=1)
        rep = jnp.broadcast_to(dq[None], (2 * DIFF_HEADS, dt, DIFF_WIDTH)).reshape(rows_dif, DIFF_WIDTH)
        col = lax.broadcasted_iota(jnp.int32, (rows_dif, DIFF_WIDTH), 1)
        own = (col // LANES == head_d) & ((col % LANES) // DIFF_HEAD_DIM == row // (DIFF_HEADS * dt))
        qbd[...] = jnp.where(own, rep, 0.0).astype(BF16)

    for cp in copies(g, slot):
        cp.wait()

    q = qcat_ref[...].reshape(rows_mla, QCAT)
    q_lat = q[:, :KV_LORA_RANK].astype(BF16)
    q_rope = q[:, KV_LORA_RANK:KV_LORA_RANK + MLA_ROPE_DIM].astype(BF16)

    lat = lat_buf[slot].astype(BF16)
    kr_t = jnp.concatenate([kr_buf[slot, j] for j in range(chunk_pages)], axis=1).astype(BF16)
    s = _dot_nt(q_lat, lat) + _dot(q_rope, kr_t)
    _online_update(s, lat, m_mla, l_mla, acc_mla)

    kd = heads_to_lanes(dk_buf.at[slot])
    vd = heads_to_lanes(dv_buf.at[slot])
    kcol = lax.broadcasted_iota(jnp.int32, (rows_dif, ck), 1)
    dist = (past + tok_d - c * ck) - kcol
    s = _dot_nt(qbd[...], kd) - slope * dist.astype(F32)
    _online_update(s, vd, m_dif, l_dif, acc_dif)

    @pl.when(c == nc - 1)
    def _():
        nlat = nlat_ref[...].astype(BF16)
        kcol_m = lax.broadcasted_iota(jnp.int32, (rows_mla, dt), 1)
        tok_m = lax.broadcasted_iota(jnp.int32, (rows_mla, dt), 0) % dt
        s = _dot_nt(q_lat, nlat) + _dot_nt(q_rope, nkr_ref[...].astype(BF16))
        s = jnp.where(kcol_m <= tok_m, s, NEG_INF)
        _online_update(s, nlat, m_mla, l_mla, acc_mla)
        kcol_n = lax.broadcasted_iota(jnp.int32, (rows_dif, dt), 1)
        s = _dot_nt(qbd[...], ndk_ref[...].astype(BF16)) - slope * (tok_d - kcol_n).astype(F32)
        s = jnp.where(kcol_n <= tok_d, s, NEG_INF)
        _online_update(s, ndv_ref[...].astype(BF16), m_dif, l_dif, acc_dif)

        lam = _lambda(lam_ref, lam_init)
        for hh in range(MLA_HEADS):
            rows = slice(hh * dt, (hh + 1) * dt)
            mla_ref[:, hh * KV_LORA_RANK:(hh + 1) * KV_LORA_RANK] = acc_mla[rows, :] * (1.0 / l_mla[rows, :])
        half = DIFF_HEADS * dt
        for hh in range(DIFF_HEADS):
            r1 = slice(hh * dt, (hh + 1) * dt)
            r2 = slice(half + hh * dt, half + (hh + 1) * dt)
            cols = slice(hh * LANES, (hh + 1) * LANES)
            o1 = acc_dif[r1, cols] * (1.0 / l_dif[r1, :])
            o2 = acc_dif[r2, cols] * (1.0 / l_dif[r2, :])
            dif_ref[:, cols] = o1 - lam * o2


def _sample_attn(page_table, qcat, dqa, nlat, nkr, ndk, ndv, lam_rows, pool_lat, pool_kr, pool_dk, pool_dv,
                 lam_init, nb, dt, chunk_pages):
    n_pages = page_table.shape[1]
    page = pool_lat.shape[1]
    assert n_pages % chunk_pages == 0
    nc = n_pages // chunk_pages
    past = n_pages * page
    t = nb * dt
    tok = lambda b, c, pt: (b, 0)
    any_spec = pl.BlockSpec(memory_space=pl.ANY)
    kern = functools.partial(_sample_attn_kernel, lam_init=lam_init, n_pages=n_pages,
                             chunk_pages=chunk_pages, page=page, past=past)
    grid_spec = pltpu.PrefetchScalarGridSpec(
        num_scalar_prefetch=1,
        grid=(nb, nc),
        in_specs=[
            pl.BlockSpec((MLA_HEADS, dt, QCAT), lambda b, c, pt: (0, b, 0)),
            pl.BlockSpec((dt, DIFF_HEADS * DAUG), tok),
            pl.BlockSpec((dt, KV_LORA_RANK), tok),
            pl.BlockSpec((dt, MLA_ROPE_DIM), tok),
            pl.BlockSpec((dt, DIFF_WIDTH), tok),
            pl.BlockSpec((dt, DIFF_WIDTH), tok),
            pl.BlockSpec((4, DIFF_HEAD_DIM), lambda b, c, pt: (0, 0)),
            any_spec, any_spec, any_spec, any_spec,
        ],
        out_specs=[
            pl.BlockSpec((dt, MLA_HEADS * KV_LORA_RANK), tok),
            pl.BlockSpec((dt, DIFF_WIDTH), tok),
        ],
        scratch_shapes=[
            pltpu.VMEM((2, chunk_pages * page, KV_LORA_RANK), F32),
            pltpu.VMEM((2, chunk_pages, MLA_ROPE_DIM, page), F32),
            pltpu.VMEM((2, chunk_pages * page * DIFF_HEADS, LANES), F32),
            pltpu.VMEM((2, chunk_pages * page * DIFF_HEADS, LANES), F32),
            pltpu.SemaphoreType.DMA((2, 4)),
            pltpu.VMEM((2 * DIFF_HEADS * dt, DIFF_WIDTH), BF16),
            pltpu.VMEM((MLA_HEADS * dt, 1), F32),
            pltpu.VMEM((MLA_HEADS * dt, 1), F32),
            pltpu.VMEM((MLA_HEADS * dt, KV_LORA_RANK), F32),
            pltpu.VMEM((2 * DIFF_HEADS * dt, 1), F32),
            pltpu.VMEM((2 * DIFF_HEADS * dt, 1), F32),
            pltpu.VMEM((2 * DIFF_HEADS * dt, DIFF_WIDTH), F32),
        ],
    )
    return pl.pallas_call(
        kern,
        grid_spec=grid_spec,
        out_shape=[
            jax.ShapeDtypeStruct((t, MLA_HEADS * KV_LORA_RANK), F32),
            jax.ShapeDtypeStruct((t, DIFF_WIDTH), F32),
        ],
        compiler_params=_cparams(("arbitrary", "arbitrary")),
        name="sample_attn",
    )(page_table.reshape(-1), qcat, dqa, nlat, nkr, ndk, ndv, lam_rows, pool_lat, pool_kr, pool_dk, pool_dv)


def _mix_out_kernel(lat_ref, dif_ref, x_ref, wuv_ref, gs_ref, wo_ref, gm_ref, wmq_ref,
                    x1_ref, qm_ref, *, lam_init):
    parts = []
    for hh in range(MLA_HEADS):
        lat = lat_ref[:, hh * KV_LORA_RANK:(hh + 1) * KV_LORA_RANK].astype(BF16)
        parts.append(_dot(lat, wuv_ref[hh]))
    gs = gs_ref[...]
    for hh in range(DIFF_HEADS):
        parts.append(_rms(dif_ref[:, hh * LANES:(hh + 1) * LANES], gs) * (1.0 - lam_init))
    mix = jnp.concatenate(parts, axis=1).astype(BF16)
    x1 = x_ref[...] + _dot(mix, wo_ref[...])
    x1_ref[...] = x1
    hq = _rms(x1, gm_ref[...]).astype(BF16)
    qm_ref[...] = (_dot(hq, wmq_ref[...]) * MEM_HEAD_DIM ** -0.5).astype(qm_ref.dtype)


def _mix_out(mla_lat, diff_o, x, wuv, subln, wo, gm, wmq, lam_init, dt, tm):
    t, d = x.shape
    row = lambda i: (i, 0)
    full = lambda i: (0, 0)
    return pl.pallas_call(
        functools.partial(_mix_out_kernel, lam_init=lam_init),
        grid=(t // tm,),
        in_specs=[
            pl.BlockSpec((tm, MLA_HEADS * KV_LORA_RANK), row),
            pl.BlockSpec((tm, DIFF_WIDTH), row),
            pl.BlockSpec((tm, d), row),
            pl.BlockSpec(wuv.shape, lambda i: (0, 0, 0)),
            pl.BlockSpec((1, LANES), full),
            pl.BlockSpec(wo.shape, full),
            pl.BlockSpec((1, d), full),
            pl.BlockSpec(wmq.shape, full),
        ],
        out_specs=[pl.BlockSpec((tm, d), row), pl.BlockSpec((tm, d), row)],
        out_shape=[jax.ShapeDtypeStruct((t, d), F32), jax.ShapeDtypeStruct((t, d), dt)],
        compiler_params=_cparams(("parallel",)),
        name="mix_out",
    )(mla_lat, diff_o, x, wuv, subln.reshape(1, -1), wo, gm.reshape(1, d), wmq)


def _mem_kv_kernel(mem_ref, g_ref, wk_ref, wv_ref, k_ref, v_ref):
    mn = _rms(mem_ref[...], g_ref[...]).astype(BF16)
    k_ref[...] = _dot(mn, wk_ref[...])
    v_ref[...] = _dot(mn, wv_ref[...])


def _mem_kv(mem, g, wk, wv, tm):
    t, d = mem.shape
    row = lambda i: (i, 0)
    full = lambda i: (0, 0)
    return pl.pallas_call(
        _mem_kv_kernel,
        grid=(t // tm,),
        in_specs=[pl.BlockSpec((tm, d), row), pl.BlockSpec((1, d), full),
                  pl.BlockSpec(wk.shape, full), pl.BlockSpec(wv.shape, full)],
        out_specs=[pl.BlockSpec((tm, d), row), pl.BlockSpec((tm, d), row)],
        out_shape=[jax.ShapeDtypeStruct((t, d), F32), jax.ShapeDtypeStruct((t, d), F32)],
        compiler_params=_cparams(("parallel",)),
        name="mem_kv",
    )(mem, g.reshape(1, d), wk, wv)


MEM_LANE_BLOCKS = MEM_HEAD_DIM // LANES


def _mem_attn_kernel(q_ref, k_ref, v_ref, o_ref, *, head_rows):
    def head(ref, hh):
        if not head_rows:
            return ref[0, :, hh * MEM_HEAD_DIM:(hh + 1) * MEM_HEAD_DIM].astype(BF16)
        per_tok = MEM_LANE_BLOCKS * MEM_HEADS
        m = ref.shape[1] // per_tok
        return jnp.concatenate(
            [ref.at[0][pl.ds(lb * MEM_HEADS + hh, m, stride=per_tok), :] for lb in range(MEM_LANE_BLOCKS)],
            axis=1).astype(BF16)

    for hh in range(MEM_HEADS):
        cols = slice(hh * MEM_HEAD_DIM, (hh + 1) * MEM_HEAD_DIM)
        q = q_ref[:, cols].astype(BF16)
        k = head(k_ref, hh)
        v = head(v_ref, hh)
        s = _dot_nt(q, k)
        p = jnp.exp(s - jnp.max(s, axis=-1, keepdims=True))
        o = _dot(p.astype(BF16), v) * (1.0 / jnp.sum(p, axis=-1, keepdims=True))
        o_ref[:, cols] = o.astype(o_ref.dtype)


def _mem_attn(qm, mk, mv, nb, rows_per_batch, tq, dt, head_rows=False):
    t, d = qm.shape
    nqb = rows_per_batch // tq
    kv_block = (1,) + mk.shape[1:]
    return pl.pallas_call(
        functools.partial(_mem_attn_kernel, head_rows=head_rows),
        grid=(nb, nqb),
        in_specs=[
            pl.BlockSpec((tq, d), lambda b, i: (b * nqb + i, 0)),
            pl.BlockSpec(kv_block, lambda b, i: (b, 0, 0)),
            pl.BlockSpec(kv_block, lambda b, i: (b, 0, 0)),
        ],
        out_specs=pl.BlockSpec((tq, d), lambda b, i: (b * nqb + i, 0)),
        out_shape=jax.ShapeDtypeStruct((t, d), dt),
        compiler_params=_cparams(("parallel", "parallel")),
        name="mem_attn",
    )(qm, mk, mv)


def _post_attn_kernel(om_ref, x1_ref, wmo_ref, gf_ref, wr_ref, br_ref, cnt_in_ref,
                      x2_ref, hn_ref, idx_ref, gate_ref, rank_ref, cnt_ref, carry):
    tm = om_ref.shape[0]

    @pl.when(pl.program_id(0) == 0)
    def _():
        carry[...] = cnt_in_ref[...]

    x2 = x1_ref[...] + _dot(om_ref[...].astype(BF16), wmo_ref[...])
    x2_ref[...] = x2
    hn = _rms(x2, gf_ref[...])
    _rows_to_tiles(hn_ref, hn)
    logits = jnp.dot(hn, wr_ref[...], preferred_element_type=F32, precision=lax.Precision.HIGHEST) + br_ref[...]
    lane_e = lax.broadcasted_iota(jnp.int32, (tm, N_EXPERTS), 1)
    lane_o = lax.broadcasted_iota(jnp.int32, (tm, LANES), 1)
    work = logits
    vals, hots = [], []
    idx_out = jnp.zeros((tm, LANES), jnp.int32)
    for k in range(TOP_K):
        mx = jnp.max(work, axis=-1, keepdims=True)
        idx = jnp.min(jnp.where(work == mx, lane_e, N_EXPERTS), axis=-1, keepdims=True)
        hot = lane_e == idx
        vals.append(mx)
        hots.append(hot)
        idx_out = jnp.where(lane_o == k, idx, idx_out)
        work = jnp.where(hot, -jnp.inf, work)
    es = [jnp.exp(v - vals[0]) for v in vals]
    den = es[0] + es[1] + es[2] + es[3]
    gate_out = jnp.zeros((tm, LANES), F32)
    for k in range(TOP_K):
        gate_out = jnp.where(lane_o == k, es[k] * (1.0 / den), gate_out)
    multi = (hots[0] | hots[1] | hots[2] | hots[3]).astype(F32)
    r = lax.broadcasted_iota(jnp.int32, (tm, tm), 0)
    cc = lax.broadcasted_iota(jnp.int32, (tm, tm), 1)
    below = (cc < r).astype(BF16)
    base = carry[...] + _dot(below, multi.astype(BF16))
    rank_out = jnp.zeros((tm, LANES), jnp.int32)
    for k in range(TOP_K):
        rk = jnp.sum(jnp.where(hots[k], base, 0.0), axis=-1, keepdims=True).astype(jnp.int32)
        rank_out = jnp.where(lane_o == k, rk, rank_out)
    idx_ref[...] = idx_out
    gate_ref[...] = gate_out
    rank_ref[...] = rank_out
    new = carry[...] + jnp.sum(multi, axis=0, keepdims=True)
    carry[...] = new
    cnt_ref[...] = new


def _post_attn(om, x1, wmo, gf, wr, br, cnt_in, tm):
    t, d = x1.shape
    row = lambda i: (i, 0)
    full = lambda i: (0, 0)
    return pl.pallas_call(
        _post_attn_kernel,
        grid=(t // tm,),
        in_specs=[
            pl.BlockSpec((tm, d), row), pl.BlockSpec((tm, d), row),
            pl.BlockSpec(wmo.shape, full), pl.BlockSpec((1, d), full),
            pl.BlockSpec(wr.shape, full), pl.BlockSpec((1, N_EXPERTS), full),
            pl.BlockSpec((1, N_EXPERTS), full),
        ],
        out_specs=[
            pl.BlockSpec((tm, d), row), pl.BlockSpec((tm * ROW_SLABS, LANES), row),
            pl.BlockSpec((tm, LANES), row), pl.BlockSpec((tm, LANES), row), pl.BlockSpec((tm, LANES), row),
            pl.BlockSpec((1, N_EXPERTS), full),
        ],
        out_shape=[
            jax.ShapeDtypeStruct((t, d), F32), jax.ShapeDtypeStruct((t * ROW_SLABS, LANES), F32),
            jax.ShapeDtypeStruct((t, LANES), jnp.int32), jax.ShapeDtypeStruct((t, LANES), F32),
            jax.ShapeDtypeStruct((t, LANES), jnp.int32),
            jax.ShapeDtypeStruct((1, N_EXPERTS), F32),
        ],
        scratch_shapes=[pltpu.VMEM((1, N_EXPERTS), F32)],
        compiler_params=_cparams(("arbitrary",)),
        name="post_attn",
    )(om, x1, wmo, gf.reshape(1, d), wr, br.reshape(1, N_EXPERTS), cnt_in)


MXU_TILE = 256
DMA_LOOP_UNROLL = 2


def _deinterleave_matrix():
    p = np.zeros((MXU_TILE, MXU_TILE), np.float32)
    half = MXU_TILE // 2
    for i in range(half):
        p[2 * i, i] = 1.0
        p[2 * i + 1, half + i] = 1.0
    return jnp.asarray(p, BF16)


def _dispatch_kernel(dest_ref, hn_ref, xs_in_ref, xs_ref, sem):
    del xs_in_ref
    tm = hn_ref.shape[0] // ROW_SLABS
    base = pl.program_id(0) * tm * TOP_K

    def copies(t):
        src = pl.multiple_of(t * ROW_SLABS, ROW_SLABS)
        out = []
        for k in range(TOP_K):
            dst = pl.multiple_of(dest_ref[base + t * TOP_K + k] * ROW_SLABS, ROW_SLABS)
            out.append(pltpu.make_async_copy(hn_ref.at[pl.ds(src, ROW_SLABS)], xs_ref.at[pl.ds(dst, ROW_SLABS)], sem))
        return out

    def start(t, carry):
        for cp in copies(t):
            cp.start()
        return carry

    def wait(t, carry):
        for cp in copies(t):
            cp.wait()
        return carry

    lax.fori_loop(0, tm, start, 0, unroll=DMA_LOOP_UNROLL)
    lax.fori_loop(0, tm, wait, 0, unroll=DMA_LOOP_UNROLL)


def _dispatch(dest_flat, hn, xs, tm):
    t = hn.shape[0] // ROW_SLABS
    grid_spec = pltpu.PrefetchScalarGridSpec(
        num_scalar_prefetch=1,
        grid=(t // tm,),
        in_specs=[pl.BlockSpec((tm * ROW_SLABS, LANES), lambda i, dest: (i, 0)), pl.BlockSpec(memory_space=pl.ANY)],
        out_specs=pl.BlockSpec(memory_space=pl.ANY),
        scratch_shapes=[pltpu.SemaphoreType.DMA(())],
    )
    return pl.pallas_call(
        _dispatch_kernel,
        grid_spec=grid_spec,
        out_shape=jax.ShapeDtypeStruct(xs.shape, xs.dtype),
        input_output_aliases={2: 0},
        compiler_params=pltpu.CompilerParams(dimension_semantics=("arbitrary",), vmem_limit_bytes=VMEM_LIMIT,
                                             has_side_effects=True),
        name="dispatch",
    )(dest_flat, hn, xs)


def _experts_kernel(te_ref, tv_ref, tf_ref, x_ref, w1_ref, w2_ref, perm_ref, b1g_ref, b1l_ref, b2_ref, y_ref,
                    w1g, w1l, w2b):
    g = pl.program_id(0)

    @pl.when(tf_ref[g] > 0)
    def _():
        half = MXU_TILE // 2
        perm = perm_ref[...]
        for cb in range(w1_ref.shape[2] // MXU_TILE):
            blk = w1_ref[0, :, cb * MXU_TILE:(cb + 1) * MXU_TILE].astype(BF16)
            r = _dot(blk, perm)
            w1g[:, cb * half:(cb + 1) * half] = r[:, :half].astype(BF16)
            w1l[:, cb * half:(cb + 1) * half] = r[:, half:].astype(BF16)
        w2b[...] = w2_ref[0].astype(BF16)

    @pl.when(tv_ref[g] > 0)
    def _():
        x = _tiles_to_rows(x_ref, x_ref.shape[0] // ROW_SLABS).astype(BF16)
        glu = jnp.minimum(_dot(x, w1g[...]) + b1g_ref[0], SWIGLU_LIMIT)
        lin = jnp.clip(_dot(x, w1l[...]) + b1l_ref[0], -SWIGLU_LIMIT, SWIGLU_LIMIT)
        a = glu * jax.nn.sigmoid(SWIGLU_ALPHA * glu) * (lin + 1.0)
        _rows_to_tiles(y_ref, _dot(a.astype(BF16), w2b[...]) + b2_ref[0])

    @pl.when(tv_ref[g] == 0)
    def _():
        y_ref[...] = jnp.zeros(y_ref.shape, F32)


def _experts(tile_expert, tile_valid, tile_first, xs, w1, w2, b1g, b1l, b2, tmoe):
    r = xs.shape[0] // ROW_SLABS
    d, f2 = w1.shape[1:]
    f = f2 // 2
    wmap = lambda g, te, tv, tf: (te[g], 0, 0)
    grid_spec = pltpu.PrefetchScalarGridSpec(
        num_scalar_prefetch=3,
        grid=(r // tmoe,),
        in_specs=[
            pl.BlockSpec((tmoe * ROW_SLABS, LANES), lambda g, te, tv, tf: (g, 0)),
            pl.BlockSpec((1, d, f2), wmap), pl.BlockSpec((1, f, d), wmap),
            pl.BlockSpec((MXU_TILE, MXU_TILE), lambda g, te, tv, tf: (0, 0)),
            pl.BlockSpec((1, 1, f), wmap), pl.BlockSpec((1, 1, f), wmap), pl.BlockSpec((1, 1, d), wmap),
        ],
        out_specs=pl.BlockSpec((tmoe * ROW_SLABS, LANES), lambda g, te, tv, tf: (g, 0)),
        scratch_shapes=[pltpu.VMEM((d, f), BF16), pltpu.VMEM((d, f), BF16), pltpu.VMEM((f, d), BF16)],
    )
    return pl.pallas_call(
        _experts_kernel,
        grid_spec=grid_spec,
        out_shape=jax.ShapeDtypeStruct(xs.shape, F32),
        compiler_params=_cparams(("arbitrary",)),
        name="moe_experts",
    )(tile_expert, tile_valid, tile_first, xs, w1, w2, _deinterleave_matrix(), b1g, b1l, b2)


def _combine_kernel(dest_ref, x2_ref, gate_ref, gf_ref, ys_hbm, y_ref, buf, sems):
    tm = x2_ref.shape[0]
    i = pl.program_id(0)
    slot = i % 2

    def copies(tile, sl, t):
        dst = pl.multiple_of(t * ROW_SLABS, ROW_SLABS)
        out = []
        for k in range(TOP_K):
            src = pl.multiple_of(dest_ref[(tile * tm + t) * TOP_K + k] * ROW_SLABS, ROW_SLABS)
            out.append(pltpu.make_async_copy(ys_hbm.at[pl.ds(src, ROW_SLABS)],
                                             buf.at[sl, k, pl.ds(dst, ROW_SLABS)], sems.at[sl]))
        return out

    def start_tile(tile, sl):
        def body(t, carry):
            for cp in copies(tile, sl, t):
                cp.start()
            return carry
        lax.fori_loop(0, tm, body, 0, unroll=DMA_LOOP_UNROLL)

    @pl.when(i == 0)
    def _():
        start_tile(i, slot)

    @pl.when(i + 1 < pl.num_programs(0))
    def _():
        start_tile(i + 1, 1 - slot)

    def wait(t, carry):
        for cp in copies(i, slot, t):
            cp.wait()
        return carry

    lax.fori_loop(0, tm, wait, 0, unroll=DMA_LOOP_UNROLL)
    gate = gate_ref[...]
    parts = []
    for s in range(ROW_SLABS):
        acc = x2_ref[:, s * LANES:(s + 1) * LANES]
        for k in range(TOP_K):
            acc = acc + gate[:, k:k + 1] * buf.at[slot, k][pl.ds(s, tm, stride=ROW_SLABS), :]
        parts.append(acc)
    y_ref[...] = _rms(jnp.concatenate(parts, axis=1), gf_ref[...])


def _combine(dest_flat, x2, gate, final_norm, ys, tm):
    t, d = x2.shape
    grid_spec = pltpu.PrefetchScalarGridSpec(
        num_scalar_prefetch=1,
        grid=(t // tm,),
        in_specs=[
            pl.BlockSpec((tm, d), lambda i, dest: (i, 0)),
            pl.BlockSpec((tm, LANES), lambda i, dest: (i, 0)),
            pl.BlockSpec((1, d), lambda i, dest: (0, 0)),
            pl.BlockSpec(memory_space=pl.ANY),
        ],
        out_specs=pl.BlockSpec((tm, d), lambda i, dest: (i, 0)),
        scratch_shapes=[pltpu.VMEM((2, TOP_K, tm * ROW_SLABS, LANES), F32), pltpu.SemaphoreType.DMA((2,))],
    )
    return pl.pallas_call(
        _combine_kernel,
        grid_spec=grid_spec,
        out_shape=jax.ShapeDtypeStruct((t, d), F32),
        compiler_params=_cparams(("arbitrary",)),
        name="combine",
    )(dest_flat, x2, gate, final_norm.reshape(1, d), ys)


def _slot_plan(counts, n_tiles, tmoe):
    counts = counts.reshape(-1).astype(jnp.int32)
    tiles = (counts + tmoe - 1) // tmoe
    ends = jnp.cumsum(tiles)
    row_start = (ends - tiles) * tmoe
    g = jnp.arange(n_tiles, dtype=jnp.int32)
    total = ends[-1]
    last_expert = jnp.max(jnp.where(counts > 0, jnp.arange(N_EXPERTS, dtype=jnp.int32), 0))
    te = jnp.sum((g[:, None] >= ends[None, :]).astype(jnp.int32), axis=1)
    tile_expert = jnp.where(g < total, jnp.minimum(te, N_EXPERTS - 1), last_expert).astype(jnp.int32)
    tile_valid = (g < total).astype(jnp.int32)
    changed = jnp.concatenate([jnp.ones((1,), jnp.bool_), tile_expert[1:] != tile_expert[:-1]])
    tile_first = (changed & (g < total)).astype(jnp.int32)
    return row_start, tile_expert, tile_valid, tile_first


TM_TOKENS = 256
TQ_ATTN = 512
TM_ROUTE = 256
TM_EXPERT = 256
CHUNK_PAGES = 16


def kernel(x_prompt, x_sample, cache_mla_latent, cache_mla_krope, cache_diff_k, cache_diff_v, cache_mem_k, cache_mem_v, page_table, mem_prompt, attn_norm, w_in, q_norm, w_uq, kv_norm, w_ukv, lambda_q1, lambda_k1, lambda_q2, lambda_k2, subln_norm, w_o, mem_q_norm, mem_kv_norm, w_mq, w_mk, w_mv, w_mo, ffn_norm, w_router, b_router, w_mlp1, b_mlp1, w_mlp2, b_mlp2, final_norm):
    depth = w_in.shape[0]
    assert depth == 1
    for h in range(DIFF_HEADS):
        assert math.frexp(_alibi_slope(h))[0] == 0.5
    l = 0
    lam_init = 0.8 - 0.6 * math.exp(-0.3 * l)
    nbp, seq, d = x_prompt.shape
    nbs, dseq, _ = x_sample.shape
    n_phys, page = cache_mla_latent.shape[1:3]
    n_pages = page_table.shape[1]
    past = n_pages * page
    mem_len = mem_prompt.shape[1]
    assert seq <= POS_RADIX * POS_RADIX and page == LANES and d == D_MODEL
    tp, ts = nbp * seq, nbs * dseq

    wa, wq, wuk, wuv = _proj_weights(w_in[l], w_uq[l], w_ukv[l])
    lam_rows = jnp.stack([lambda_q1[l], lambda_k1[l], lambda_q2[l], lambda_k2[l]]).astype(F32)
    wo = w_o[l].astype(BF16)
    wmq = w_mq[l].astype(BF16)
    wmo = w_mo[l].astype(BF16)

    xp = x_prompt.reshape(tp, d)
    tm_p = min(TM_TOKENS, seq)
    (p_ckv, p_kr, p_dk, p_dv, p_qcat, p_kvcat, p_dqa, p_dk1, p_dk2, p_dva) = _proj(
        xp, jnp.arange(seq, dtype=jnp.int32), nbp, attn_norm[l], wa, q_norm[l], wq, kv_norm[l], wuk, BF16, tm_p)
    p_lat, p_dif = _prompt_attn(p_qcat, p_kvcat, p_dqa, p_dk1, p_dk2, p_dva, lam_rows, lam_init,
                                nbp, seq, min(TQ_ATTN, seq))
    p_x1, p_qm = _mix_out(p_lat, p_dif, xp, wuv, subln_norm[l], wo, mem_q_norm[l], wmq, lam_init, BF16, tm_p)
    mk, mv = _mem_kv(mem_prompt.reshape(nbp * mem_len, d), mem_kv_norm[l], w_mk[l].astype(BF16),
                     w_mv[l].astype(BF16), min(TM_TOKENS, nbp * mem_len))
    p_om = _mem_attn(p_qm, mk.reshape(nbp, mem_len, d), mv.reshape(nbp, mem_len, d), nbp, seq, tm_p, BF16)

    xs_tok = x_sample.reshape(ts, d)
    tm_s = min(TM_TOKENS, ts)
    pos_s = past + jnp.arange(dseq, dtype=jnp.int32)
    (s_ckv, s_kr, s_dk, s_dv, s_qcat, _, s_dqa, _, _, _) = _proj(
        xs_tok, jnp.tile(pos_s, tm_s // dseq), ts // tm_s, attn_norm[l], wa, q_norm[l], wq, kv_norm[l], wuk,
        F32, tm_s)
    s_lat, s_dif = _sample_attn(
        page_table, s_qcat, s_dqa, s_ckv, s_kr, s_dk, s_dv, lam_rows,
        cache_mla_latent.reshape(n_phys, page, KV_LORA_RANK),
        jnp.swapaxes(cache_mla_krope.reshape(n_phys, page, MLA_ROPE_DIM), 1, 2),
        cache_diff_k.reshape(n_phys, page * DIFF_HEADS, LANES), cache_diff_v.reshape(n_phys, page * DIFF_HEADS, LANES),
        lam_init, nbs, dseq, min(CHUNK_PAGES, n_pages))
    s_x1, s_qm = _mix_out(s_lat, s_dif, xs_tok, wuv, subln_norm[l], wo, mem_q_norm[l], wmq, lam_init, F32, tm_s)
    def mem_head_rows(c):
        c = c.reshape(nbs, mem_len, MEM_HEADS, MEM_LANE_BLOCKS, LANES)
        return jnp.transpose(c, (0, 1, 3, 2, 4)).reshape(nbs, mem_len * MEM_LANE_BLOCKS * MEM_HEADS, LANES)

    s_om = _mem_attn(s_qm, mem_head_rows(cache_mem_k), mem_head_rows(cache_mem_v), nbs, dseq, dseq, F32,
                     head_rows=True)

    tr_p, tr_s = min(TM_ROUTE, tp), min(TM_ROUTE, ts)
    zero_cnt = jnp.zeros((1, N_EXPERTS), F32)
    p_x2, p_hn, p_idx, p_gate, p_rank, cnt_p = _post_attn(p_om, p_x1, wmo, ffn_norm[l], w_router[l], b_router[l],
                                                          zero_cnt, tr_p)
    s_x2, s_hn, s_idx, s_gate, s_rank, cnt = _post_attn(s_om, s_x1, wmo, ffn_norm[l], w_router[l], b_router[l],
                                                        cnt_p, tr_s)
    n_pairs = (tp + ts) * TOP_K
    n_tiles = (n_pairs + N_EXPERTS * (TM_EXPERT - 1)) // TM_EXPERT + 1
    row_start, tile_expert, tile_valid, tile_first = _slot_plan(cnt, n_tiles, TM_EXPERT)
    p_dest = (row_start[p_idx[:, :TOP_K]] + p_rank[:, :TOP_K]).reshape(-1)
    s_dest = (row_start[s_idx[:, :TOP_K]] + s_rank[:, :TOP_K]).reshape(-1)

    b1 = b_mlp1[l].reshape(N_EXPERTS, D_FF, 2)
    b1g = b1[:, :, 0].reshape(N_EXPERTS, 1, D_FF)
    b1l = b1[:, :, 1].reshape(N_EXPERTS, 1, D_FF)
    b2 = b_mlp2[l].reshape(N_EXPERTS, 1, d)
    slots = jnp.zeros((n_tiles * TM_EXPERT * ROW_SLABS, LANES), F32)
    slots = _dispatch(p_dest, p_hn, slots, tr_p)
    slots = _dispatch(s_dest, s_hn, slots, tr_s)
    ys = _experts(tile_expert, tile_valid, tile_first, slots, w_mlp1.reshape(w_mlp1.shape[1:]),
                  w_mlp2.reshape(w_mlp2.shape[1:]), b1g, b1l, b2, TM_EXPERT)
    y_p = _combine(p_dest, p_x2, p_gate, final_norm, ys, tr_p)
    y_s = _combine(s_dest, s_x2, s_gate, final_norm, ys, tr_s)

    dshape = (DIFF_HEADS, 2 * DIFF_HEAD_DIM)
    mshape = (MEM_HEADS, MEM_HEAD_DIM)
    return (
        y_p.reshape(nbp, seq, d),
        y_s.reshape(nbs, dseq, d),
        p_ckv.reshape(1, nbp, seq, KV_LORA_RANK),
        p_kr.reshape(1, nbp, seq, MLA_ROPE_DIM),
        p_dk.reshape(1, nbp, seq, *dshape),
        p_dv.reshape(1, nbp, seq, *dshape),
        mk.reshape(1, nbp, mem_len, *mshape),
        mv.reshape(1, nbp, mem_len, *mshape),
        s_ckv.reshape(1, nbs, dseq, KV_LORA_RANK),
        s_kr.reshape(1, nbs, dseq, MLA_ROPE_DIM),
        s_dk.reshape(1, nbs, dseq, *dshape),
        s_dv.reshape(1, nbs, dseq, *dshape),
    )
```
